```python
import jax, jax.numpy as jnp
from jax import lax
import numpy as np

D_MODEL = 1024
BATCH = 16
SEQ = 2048
DEPTH = 4

PLE_DIM = 256
N_MIXERS = 2
N_GLA_LAYERS = (DEPTH + 1) // 2
N_POOL_LAYERS = DEPTH // 2
EPS = 1e-6

GLA_HEADS = 4
GLA_KEY_DIM = D_MODEL // 2
GLA_VAL_DIM = D_MODEL
GLA_HEAD_K = GLA_KEY_DIM // GLA_HEADS
GLA_HEAD_V = GLA_VAL_DIM // GLA_HEADS
GLA_GATE_RANK = 16
GLA_GATE_NORMALIZER = 16.0
GLA_CHUNK = 64
GLA_IN_DIM = 2 * GLA_KEY_DIM + 2 * GLA_VAL_DIM + GLA_GATE_RANK

POOL_GROUPS = 4
POOL_WINDOWS = (2, 4, 8, 16)
POOL_GROUP_DIM = D_MODEL // POOL_GROUPS

MOE_GROUPS = 8
MOE_EXPERTS_PER_GROUP = 8
MOE_N_EXPERTS = MOE_GROUPS * MOE_EXPERTS_PER_GROUP
MOE_TOP_K = 2
MOE_D_FF = (3 * D_MODEL) // 8
MOE_BLOCK = 256

kernel_name = "hybrid_gla_pool_hmoe_ple"


def rmsnorm(x, g):
    xf = x.astype(jnp.float32)
    y = xf * lax.rsqrt(jnp.mean(xf * xf, axis=-1, keepdims=True) + EPS) * g.astype(jnp.float32)
    return y.astype(x.dtype)


def gla_chunked(q, k, v, gk):
    B, H, S, dk = q.shape
    dv = v.shape[-1]
    C = GLA_CHUNK
    N = S // C
    q, k, gk = (t.astype(jnp.float32).reshape(B, H, N, C, dk) for t in (q, k, gk))
    v = v.astype(jnp.float32).reshape(B, H, N, C, dv)
    b = jnp.cumsum(gk, axis=3)
    b_last = b[:, :, :, -1:, :]
    q_dec = q * jnp.exp(b)
    k_inv = k * jnp.exp(-b)
    k_end = k * jnp.exp(b_last - b)
    causal = jnp.tril(jnp.ones((C, C), dtype=bool))
    attn = jnp.where(causal, jnp.einsum('bhnik,bhnjk->bhnij', q_dec, k_inv), 0.0)
    o_intra = jnp.einsum('bhnij,bhnjv->bhniv', attn, v)
    kv = jnp.einsum('bhnck,bhncv->bhnkv', k_end, v)
    decay = jnp.exp(b_last[:, :, :, 0, :])

    def step(state, inp):
        q_c, kv_c, d_c = inp
        o_c = jnp.einsum('bhck,bhkv->bhcv', q_c, state)
        return d_c[..., None] * state + kv_c, o_c

    xs = (jnp.moveaxis(q_dec, 2, 0), jnp.moveaxis(kv, 2, 0), jnp.moveaxis(decay, 2, 0))
    _, o_inter = lax.scan(step, jnp.zeros((B, H, dk, dv), jnp.float32), xs)
    o = o_intra + jnp.moveaxis(o_inter, 0, 2)
    return o.reshape(B, H, S, dv)


def gla_mixer(xn, w_in, w_gk, b_gk, norm_g, w_out):
    B, S, _ = xn.shape
    proj = xn @ w_in
    K, V = GLA_KEY_DIM, GLA_VAL_DIM
    q, k, v, r, lr = jnp.split(proj, [K, 2 * K, 2 * K + V, 2 * K + 2 * V], axis=-1)
    gk = jax.nn.log_sigmoid((lr @ w_gk + b_gk).astype(jnp.float32)) / GLA_GATE_NORMALIZER

    def heads(t, dh):
        return t.reshape(B, S, GLA_HEADS, dh).transpose(0, 2, 1, 3)

    o = gla_chunked(heads(q, GLA_HEAD_K) * (GLA_HEAD_K ** -0.5), heads(k, GLA_HEAD_K),
                    heads(v, GLA_HEAD_V), heads(gk, GLA_HEAD_K))
    o = o.transpose(0, 2, 1, 3)
    o = rmsnorm(o, norm_g) * jax.nn.silu(r.reshape(B, S, GLA_HEADS, GLA_HEAD_V).astype(jnp.float32))
    return o.reshape(B, S, V).astype(xn.dtype) @ w_out


def pool_mixer(xn, w, b, scale):
    B, S, D = xn.shape
    xg = xn.astype(jnp.float32).reshape(B, S, POOL_GROUPS, POOL_GROUP_DIM)
    cs = jnp.concatenate([jnp.zeros((B, 1, POOL_GROUPS, POOL_GROUP_DIM), jnp.float32),
                          jnp.cumsum(xg, axis=1)], axis=1)
    win = jnp.array(POOL_WINDOWS, dtype=jnp.int32)
    t = jnp.arange(S, dtype=jnp.int32)[:, None]
    lo = jnp.maximum(t + 1 - win, 0)
    cnt = jnp.minimum(t + 1, win).astype(jnp.float32)
    grp = jnp.arange(POOL_GROUPS, dtype=jnp.int32)[None, :]
    window_sum = cs[:, 1:] - cs[:, lo, grp, :]
    pooled = (window_sum / cnt[None, :, :, None] - xg).astype(xn.dtype)
    y = jnp.einsum('bsgc,gcd->bsgd', pooled, w) + b
    return y.reshape(B, S, D) * scale


def hier_moe(xn, w_coarse, b_coarse, w_fine, b_fine, w_gate, w_up, w_down):
    B, S, D = xn.shape
    T = B * S
    xf = xn.reshape(T, D)
    lc = (xf @ w_coarse).astype(jnp.float32) + b_coarse.astype(jnp.float32)
    pc = jax.nn.softmax(lc, axis=-1)
    _, g_top = lax.top_k(lc, 1)
    p_g = jnp.take_along_axis(pc, g_top, axis=1)
    lf = ((xf @ w_fine).astype(jnp.float32) + b_fine.astype(jnp.float32))
    lf = lf.reshape(T, MOE_GROUPS, MOE_EXPERTS_PER_GROUP)
    lf_sel = jnp.take_along_axis(lf, g_top[:, :, None], axis=1)[:, 0]
    top_v, top_i = lax.top_k(lf_sel, MOE_TOP_K)
    weight = p_g * jax.nn.softmax(top_v, axis=-1)
    expert_id = g_top * MOE_EXPERTS_PER_GROUP + top_i

    A = T * MOE_TOP_K
    e_flat = expert_id.reshape(A).astype(jnp.int32)
    w_flat = weight.reshape(A)
    tok_flat = jnp.repeat(jnp.arange(T, dtype=jnp.int32), MOE_TOP_K)
    order = jnp.argsort(e_flat)
    se, st, sw = e_flat[order], tok_flat[order], w_flat[order]
    counts = jax.ops.segment_sum(jnp.ones((A,), jnp.int32), e_flat, num_segments=MOE_N_EXPERTS)
    starts = jnp.cumsum(counts) - counts
    padded = ((counts + MOE_BLOCK - 1) // MOE_BLOCK) * MOE_BLOCK
    pends = jnp.cumsum(padded)
    pstarts = pends - padded
    dest = pstarts[se] + jnp.arange(A, dtype=jnp.int32) - starts[se]
    n_blocks = -(-A // MOE_BLOCK) + MOE_N_EXPERTS
    P = n_blocks * MOE_BLOCK
    buf_tok = jnp.full((P,), T, jnp.int32).at[dest].set(st)
    buf_w = jnp.zeros((P,), jnp.float32).at[dest].set(sw)
    block_start = jnp.arange(n_blocks, dtype=jnp.int32) * MOE_BLOCK
    block_e = jnp.minimum(jnp.searchsorted(pends, block_start, side='right'),
                          MOE_N_EXPERTS - 1).astype(jnp.int32)
    x_pad = jnp.concatenate([xf, jnp.zeros((1, D), xf.dtype)], axis=0)
    xb = x_pad[buf_tok].reshape(n_blocks, MOE_BLOCK, D)

    def expert_block(args):
        xblk, e = args
        h = jax.nn.silu(xblk @ w_gate[e]) * (xblk @ w_up[e])
        return h @ w_down[e]

    yb = lax.map(expert_block, (xb, block_e)).reshape(P, D)
    yb = yb * buf_w[:, None].astype(yb.dtype)
    out = jnp.zeros((T + 1, D), yb.dtype).at[buf_tok].add(yb)[:T]
    return out.reshape(B, S, D).astype(xn.dtype)


def setup_inputs(seed: int = 0) -> dict:
    key = jax.random.key(seed)
    ks = jax.random.split(key, 24)
    f32 = jnp.float32

    def nrm(k, shape, fan_in):
        return jax.random.normal(k, shape, f32) * (fan_in ** -0.5)

    def gain(k, shape):
        return 1.0 + 0.02 * jax.random.normal(k, shape, f32)

    def bias(k, shape):
        return 0.01 * jax.random.normal(k, shape, f32)

    return {
        "x": jax.random.normal(ks[0], (BATCH, SEQ, D_MODEL), f32),
        "p": jax.random.normal(ks[1], (DEPTH, BATCH, SEQ, PLE_DIM), f32),
        "norm_mix": gain(ks[2], (DEPTH, D_MODEL)),
        "norm_ffn": gain(ks[3], (DEPTH, D_MODEL)),
        "norm_ple": gain(ks[4], (DEPTH, D_MODEL)),
        "norm_final": gain(ks[5], (D_MODEL,)),
        "gla_w_in": nrm(ks[6], (N_GLA_LAYERS, D_MODEL, GLA_IN_DIM), D_MODEL),
        "gla_w_gk": nrm(ks[7], (N_GLA_LAYERS, GLA_GATE_RANK, GLA_KEY_DIM), GLA_GATE_RANK),
        "gla_b_gk": bias(ks[8], (N_GLA_LAYERS, GLA_KEY_DIM)),
        "gla_norm": gain(ks[9], (N_GLA_LAYERS, GLA_HEAD_V)),
        "gla_w_out": nrm(ks[10], (N_GLA_LAYERS, GLA_VAL_DIM, D_MODEL), GLA_VAL_DIM),
        "pool_w": nrm(ks[11], (N_POOL_LAYERS, POOL_GROUPS, POOL_GROUP_DIM, POOL_GROUP_DIM), POOL_GROUP_DIM),
        "pool_b": bias(ks[12], (N_POOL_LAYERS, POOL_GROUPS, POOL_GROUP_DIM)),
        "pool_scale": gain(ks[13], (N_POOL_LAYERS, D_MODEL)),
        "moe_w_coarse": nrm(ks[14], (DEPTH, D_MODEL, MOE_GROUPS), D_MODEL),
        "moe_b_coarse": bias(ks[15], (DEPTH, MOE_GROUPS)),
        "moe_w_fine": nrm(ks[16], (DEPTH, D_MODEL, MOE_N_EXPERTS), D_MODEL),
        "moe_b_fine": bias(ks[17], (DEPTH, MOE_N_EXPERTS)),
        "moe_w_gate": nrm(ks[18], (DEPTH, MOE_N_EXPERTS, D_MODEL, MOE_D_FF), D_MODEL),
        "moe_w_up": nrm(ks[19], (DEPTH, MOE_N_EXPERTS, D_MODEL, MOE_D_FF), D_MODEL),
        "moe_w_down": nrm(ks[20], (DEPTH, MOE_N_EXPERTS, MOE_D_FF, D_MODEL), MOE_D_FF),
        "ple_w_gate": nrm(ks[21], (DEPTH, D_MODEL, D_MODEL), D_MODEL),
        "ple_w_proj": nrm(ks[22], (DEPTH, PLE_DIM, D_MODEL), PLE_DIM),
    }


def reference(x, p, norm_mix, norm_ffn, norm_ple, norm_final, gla_w_in, gla_w_gk, gla_b_gk, gla_norm,
              gla_w_out, pool_w, pool_b, pool_scale, moe_w_coarse, moe_b_coarse, moe_w_fine, moe_b_fine,
              moe_w_gate, moe_w_up, moe_w_down, ple_w_gate, ple_w_proj):
    h = x
    for i in range(DEPTH):
        j = i // N_MIXERS
        xn = rmsnorm(h, norm_mix[i])
        if i % N_MIXERS == 0:
            h = h + gla_mixer(xn, gla_w_in[j], gla_w_gk[j], gla_b_gk[j], gla_norm[j], gla_w_out[j])
        else:
            h = h + pool_mixer(xn, pool_w[j], pool_b[j], pool_scale[j])
        h = h + hier_moe(rmsnorm(h, norm_ffn[i]), moe_w_coarse[i], moe_b_coarse[i], moe_w_fine[i],
                         moe_b_fine[i], moe_w_gate[i], moe_w_up[i], moe_w_down[i])
        gate = jax.nn.sigmoid(rmsnorm(h, norm_ple[i]) @ ple_w_gate[i])
        h = h + gate * (p[i] @ ple_w_proj[i])
    return rmsnorm(h, norm_final)
```

```python
import functools

import jax
import jax.numpy as jnp
from jax import lax
from jax.experimental import pallas as pl
from jax.experimental.pallas import tpu as pltpu

EPS = 1e-6
GLA_HEADS = 4
GLA_GATE_RANK = 16
GLA_GATE_NORMALIZER = 16.0
GLA_CHUNK = 64
POOL_WINDOWS = (2, 4, 8, 16)
MOE_GROUPS = 8
MOE_EXPERTS_PER_GROUP = 8
MOE_N_EXPERTS = MOE_GROUPS * MOE_EXPERTS_PER_GROUP

LANES = 128
SUBLANES = 8
VMEM_LIMIT = 56 * 1024 * 1024
ROW_TILE = 512
GLA_BLOCK = 256
EXPERT_BLOCK = 256
GATHER_ROWS = 512
POOL_HALO = 16

BF16 = jnp.bfloat16
F32 = jnp.float32


def _params(*sem):
    return pltpu.CompilerParams(dimension_semantics=sem, vmem_limit_bytes=VMEM_LIMIT)


def _rms(x, g):
    return x * lax.rsqrt(jnp.mean(x * x, axis=-1, keepdims=True) + EPS) * g


def _read_token_tiles(ref, first, stride, rows):
    return jnp.concatenate(
        [ref[pl.ds(first + j, rows, stride=stride), :] for j in range(SUBLANES)], axis=-1)


def _write_token_tiles(ref, val):
    rows = val.shape[0]
    for j in range(SUBLANES):
        ref[pl.ds(j, rows, stride=SUBLANES), :] = val[:, j * LANES:(j + 1) * LANES]


def _gla_proj_kernel(h_ref, g_ref, wq_ref, wk_ref, wv_ref, wr_ref, wlr_ref, wgk_ref, bgk_ref,
                     q_ref, k_ref, v_ref, r_ref, gk_ref, *, q_scale):
    xn = _rms(h_ref[...], g_ref[...]).astype(BF16)
    q_ref[...] = (jnp.dot(xn, wq_ref[...], preferred_element_type=F32) * q_scale).astype(BF16)
    k_ref[...] = jnp.dot(xn, wk_ref[...], preferred_element_type=F32).astype(BF16)
    v_ref[...] = jnp.dot(xn, wv_ref[...], preferred_element_type=F32).astype(BF16)
    r_ref[...] = jnp.dot(xn, wr_ref[...], preferred_element_type=F32).astype(BF16)
    lr = jnp.dot(xn, wlr_ref[...], preferred_element_type=F32).astype(BF16)
    z = jnp.dot(lr, wgk_ref[...], preferred_element_type=F32) + bgk_ref[...]
    gk_ref[...] = (jnp.minimum(z, 0.0) - jnp.log(1.0 + jnp.exp(-jnp.abs(z)))) * (1.0 / GLA_GATE_NORMALIZER)


def _gla_proj(h, g, w_in, w_gk, b_gk):
    T, D = h.shape
    K = w_gk.shape[1]
    V = (w_in.shape[1] - GLA_GATE_RANK - 2 * K) // 2
    wq = w_in[:, :K].astype(BF16)
    wk = w_in[:, K:2 * K].astype(BF16)
    wv = w_in[:, 2 * K:2 * K + V].astype(BF16)
    wr = w_in[:, 2 * K + V:2 * K + 2 * V].astype(BF16)
    wlr = jnp.pad(w_in[:, 2 * K + 2 * V:], ((0, 0), (0, LANES - GLA_GATE_RANK))).astype(BF16)
    wgk = jnp.pad(w_gk, ((0, LANES - GLA_GATE_RANK), (0, 0))).astype(BF16)
    tm = min(ROW_TILE, T)
    full = lambda a: pl.BlockSpec(a.shape, lambda i: (0,) * a.ndim)
    rows = lambda n: pl.BlockSpec((tm, n), lambda i: (i, 0))
    g2 = g.reshape(1, D)
    b2 = b_gk.reshape(1, K)
    return pl.pallas_call(
        functools.partial(_gla_proj_kernel, q_scale=float((K // GLA_HEADS) ** -0.5)),
        grid=(T // tm,),
        in_specs=[rows(D), full(g2), full(wq), full(wk), full(wv), full(wr), full(wlr), full(wgk), full(b2)],
        out_specs=[rows(K), rows(K), rows(V), rows(V), rows(K)],
        out_shape=[jax.ShapeDtypeStruct((T, K), BF16), jax.ShapeDtypeStruct((T, K), BF16),
                   jax.ShapeDtypeStruct((T, V), BF16), jax.ShapeDtypeStruct((T, V), BF16),
                   jax.ShapeDtypeStruct((T, K), F32)],
        compiler_params=_params("parallel"),
        name="gla_proj",
    )(h, g2, wq, wk, wv, wr, wlr, wgk, b2)


def _gla_core_kernel(tri_ref, q_ref, k_ref, v_ref, r_ref, gk_ref, ng_ref, o_ref, state_ref, *, dk, dv):
    @pl.when(pl.program_id(1) == 0)
    def _():
        state_ref[...] = jnp.zeros_like(state_ref)

    C = GLA_CHUNK
    n_chunks = q_ref.shape[0] // C
    gk = gk_ref[...]
    gk_hi = gk.astype(BF16)
    gk_lo = (gk - gk_hi.astype(F32)).astype(BF16)
    tri = tri_ref[...]
    b_all = (jnp.dot(tri, gk_hi, preferred_element_type=F32)
             + jnp.dot(tri, gk_lo, preferred_element_type=F32))
    row = lax.broadcasted_iota(jnp.int32, (C, C), 0)
    col = lax.broadcasted_iota(jnp.int32, (C, C), 1)
    causal = col <= row
    ng = ng_ref[...]
    for hd in range(GLA_HEADS):
        ks = slice(hd * dk, (hd + 1) * dk)
        vs = slice(hd * dv, (hd + 1) * dv)
        for c in range(n_chunks):
            rs = slice(c * C, (c + 1) * C)
            b = b_all[rs, ks]
            b_last = b[C - 1:C, :]
            q = q_ref[rs, ks].astype(F32)
            k = k_ref[rs, ks].astype(F32)
            v = v_ref[rs, vs]
            q_dec = (q * jnp.exp(b)).astype(BF16)
            k_inv = (k * jnp.exp(-b)).astype(BF16)
            k_end = (k * jnp.exp(b_last - b)).astype(BF16)
            attn = lax.dot_general(q_dec, k_inv, (((1,), (1,)), ((), ())), preferred_element_type=F32)
            attn = jnp.where(causal, attn, 0.0).astype(BF16)
            st = state_ref[hd]
            o = jnp.dot(attn, v, preferred_element_type=F32)
            o = o + lax.dot_general(q_dec, st.astype(BF16), (((1,), (1,)), ((), ())),
                                    preferred_element_type=F32)
            upd = lax.dot_general(v, k_end, (((0,), (0,)), ((), ())), preferred_element_type=F32)
            state_ref[hd] = st * jnp.exp(b_last) + upd
            rr = r_ref[rs, vs].astype(F32)
            o_ref[rs, vs] = (_rms(o, ng) * (rr * jax.nn.sigmoid(rr))).astype(BF16)


def _gla_core(q, k, v, r, gk, norm_g, batch, seq):
    T, K = q.shape
    V = v.shape[1]
    dk, dv = K // GLA_HEADS, V // GLA_HEADS
    L = min(GLA_BLOCK, seq)
    nblk = seq // L
    idx = jnp.arange(L)
    tri = ((idx[None, :] <= idx[:, None]) & (idx[None, :] // GLA_CHUNK == idx[:, None] // GLA_CHUNK)).astype(BF16)
    ng = norm_g.reshape(1, dv)
    rows = lambda n: pl.BlockSpec((L, n), lambda b, s: (b * nblk + s, 0))
    full = lambda a: pl.BlockSpec(a.shape, lambda b, s: (0,) * a.ndim)
    return pl.pallas_call(
        functools.partial(_gla_core_kernel, dk=dk, dv=dv),
        grid=(batch, nblk),
        in_specs=[full(tri), rows(K), rows(K), rows(V), rows(V), rows(K), full(ng)],
        out_specs=rows(V),
        out_shape=jax.ShapeDtypeStruct((T, V), BF16),
        scratch_shapes=[pltpu.VMEM((GLA_HEADS, dv, dk), F32)],
        compiler_params=_params("parallel", "arbitrary"),
        name="gla_core",
    )(tri, q, k, v, r, gk, ng)


def _proj_residual_kernel(h_ref, x_ref, w_ref, o_ref):
    o_ref[...] = h_ref[...] + jnp.dot(x_ref[...], w_ref[...], preferred_element_type=F32)


def _proj_residual(h, x, w):
    T, D = h.shape
    tm = min(ROW_TILE, T)
    wb = w.astype(BF16)
    return pl.pallas_call(
        _proj_residual_kernel,
        grid=(T // tm,),
        in_specs=[pl.BlockSpec((tm, D), lambda i: (i, 0)), pl.BlockSpec((tm, x.shape[1]), lambda i: (i, 0)),
                  pl.BlockSpec(wb.shape, lambda i: (0, 0))],
        out_specs=pl.BlockSpec((tm, D), lambda i: (i, 0)),
        out_shape=jax.ShapeDtypeStruct((T, D), F32),
        compiler_params=_params("parallel"),
        name="gla_out_proj",
    )(h, x, wb)


def _gla_layer(h, g, w_in, w_gk, b_gk, norm_g, w_out, batch, seq):
    q, k, v, r, gk = _gla_proj(h, g, w_in, w_gk, b_gk)
    og = _gla_core(q, k, v, r, gk, norm_g, batch, seq)
    return _proj_residual(h, og, w_out)


def _pool_kernel(h_ref, g_ref, w_ref, b_ref, sc_ref, o_ref, carry_ref, *, ts):
    s = pl.program_id(1)

    @pl.when(s == 0)
    def _():
        carry_ref[...] = jnp.zeros_like(carry_ref)

    h = h_ref[...]
    xn = _rms(h, g_ref[...])
    ext = jnp.concatenate([carry_ref[...], xn], axis=0)
    carry_ref[...] = xn[ts - POOL_HALO:, :]
    gd = w_ref.shape[1]
    t = s * ts + lax.broadcasted_iota(jnp.int32, (ts, gd), 0)
    for gi, win in enumerate(POOL_WINDOWS):
        cols = slice(gi * gd, (gi + 1) * gd)
        acc = ext[:, cols]
        d = 1
        while d < win:
            acc = acc[d:, :] + acc[:-d, :]
            d *= 2
        start = POOL_HALO - (win - 1)
        wsum = acc[start:start + ts, :]
        cnt = jnp.minimum(t + 1, win).astype(F32)
        pooled = (wsum / cnt - xn[:, cols]).astype(BF16)
        y = jnp.dot(pooled, w_ref[gi], preferred_element_type=F32) + b_ref[:, cols]
        o_ref[:, cols] = h[:, cols] + y * sc_ref[:, cols]


def _pool_layer(h, g, w, b, scale, batch, seq):
    T, D = h.shape
    ts = min(ROW_TILE, seq)
    nblk = seq // ts
    wb = w.astype(BF16)
    g2, b2, s2 = g.reshape(1, D), b.reshape(1, D), scale.reshape(1, D)
    rows = pl.BlockSpec((ts, D), lambda bb, s: (bb * nblk + s, 0))
    full = lambda a: pl.BlockSpec(a.shape, lambda bb, s: (0,) * a.ndim)
    return pl.pallas_call(
        functools.partial(_pool_kernel, ts=ts),
        grid=(batch, nblk),
        in_specs=[rows, full(g2), full(wb), full(b2), full(s2)],
        out_specs=rows,
        out_shape=jax.ShapeDtypeStruct((T, D), F32),
        scratch_shapes=[pltpu.VMEM((POOL_HALO, D), F32)],
        compiler_params=_params("parallel", "arbitrary"),
        name="pool_mixer",
    )(h, g2, wb, b2, s2)


def _route_kernel(h_ref, g_ref, w_ref, b_ref, xn_ref, route_ref):
    xn = _rms(h_ref[...], g_ref[...])
    _write_token_tiles(xn_ref, xn)
    logits = jnp.dot(xn, w_ref[...], preferred_element_type=F32, precision=lax.Precision.HIGHEST) + b_ref[...]
    lane = lax.broadcasted_iota(jnp.int32, logits.shape, 1)
    neg = jnp.float32(-jnp.inf)
    big = jnp.int32(LANES)
    is_c = lane < MOE_GROUPS
    lc = jnp.where(is_c, logits, neg)
    cmax = jnp.max(lc, axis=-1, keepdims=True)
    g_top = jnp.min(jnp.where(lc == cmax, lane, big), axis=-1, keepdims=True)
    p_g = 1.0 / jnp.sum(jnp.where(is_c, jnp.exp(logits - cmax), 0.0), axis=-1, keepdims=True)
    fine_lane = lane - MOE_GROUPS
    in_grp = (fine_lane >= g_top * MOE_EXPERTS_PER_GROUP) & (fine_lane < (g_top + 1) * MOE_EXPERTS_PER_GROUP)
    lf = jnp.where(in_grp, logits, neg)
    m1 = jnp.max(lf, axis=-1, keepdims=True)
    i1 = jnp.min(jnp.where(lf == m1, lane, big), axis=-1, keepdims=True)
    lf2 = jnp.where(lane == i1, neg, lf)
    m2 = jnp.max(lf2, axis=-1, keepdims=True)
    i2 = jnp.min(jnp.where(lf2 == m2, lane, big), axis=-1, keepdims=True)
    e2 = jnp.exp(m2 - m1)
    w1 = p_g / (1.0 + e2)
    w2 = p_g * e2 / (1.0 + e2)
    out = jnp.where(lane == 0, (i1 - MOE_GROUPS).astype(F32),
                    jnp.where(lane == 1, (i2 - MOE_GROUPS).astype(F32),
                              jnp.where(lane == 2, w1, jnp.where(lane == 3, w2, 0.0))))
    route_ref[...] = out


def _route(h, g, w_coarse, b_coarse, w_fine, b_fine):
    T, D = h.shape
    n_log = MOE_GROUPS + MOE_N_EXPERTS
    wr = jnp.pad(jnp.concatenate([w_coarse, w_fine], axis=1), ((0, 0), (0, LANES - n_log)))
    br = jnp.pad(jnp.concatenate([b_coarse, b_fine]), (0, LANES - n_log)).reshape(1, LANES)
    g2 = g.reshape(1, D)
    tm = min(ROW_TILE, T)
    full = lambda a: pl.BlockSpec(a.shape, lambda i: (0,) * a.ndim)
    return pl.pallas_call(
        _route_kernel,
        grid=(T // tm,),
        in_specs=[pl.BlockSpec((tm, D), lambda i: (i, 0)), full(g2), full(wr), full(br)],
        out_specs=[pl.BlockSpec((tm * SUBLANES, LANES), lambda i: (i, 0)),
                   pl.BlockSpec((tm, LANES), lambda i: (i, 0))],
        out_shape=[jax.ShapeDtypeStruct((T * SUBLANES, LANES), F32), jax.ShapeDtypeStruct((T, LANES), F32)],
        compiler_params=_params("parallel"),
        name="moe_route",
    )(h, g2, wr, br)


def _gather_kernel(idx_ref, src_ref, dst_ref, sem, *, rows):
    base = pl.program_id(0) * rows

    def issue(r, carry):
        pltpu.make_async_copy(src_ref.at[idx_ref[base + r]], dst_ref.at[base + r], sem).start()
        return carry

    lax.fori_loop(0, rows, issue, 0)
    pltpu.make_async_copy(src_ref.at[pl.ds(0, rows)], dst_ref.at[pl.ds(base, rows)], sem).wait()


def _gather_rows(src, idx):
    n = idx.shape[0]
    rows = min(GATHER_ROWS, n)
    return pl.pallas_call(
        functools.partial(_gather_kernel, rows=rows),
        grid_spec=pltpu.PrefetchScalarGridSpec(
            num_scalar_prefetch=1,
            grid=(n // rows,),
            in_specs=[pl.BlockSpec(memory_space=pl.ANY)],
            out_specs=pl.BlockSpec(memory_space=pl.ANY),
            scratch_shapes=[pltpu.SemaphoreType.DMA(())],
        ),
        out_shape=jax.ShapeDtypeStruct((n,) + src.shape[1:], src.dtype),
        compiler_params=pltpu.CompilerParams(dimension_semantics=("arbitrary",)),
        name="row_gather",
    )(idx, src)


def _expert_kernel(be_ref, nused_ref, x_ref, wg_ref, wu_ref, wd_ref, y_ref, *, rows):
    i = pl.program_id(0)

    @pl.when(i < nused_ref[0])
    def _():
        x = _read_token_tiles(x_ref, 0, SUBLANES, rows).astype(BF16)
        gate = jnp.dot(x, wg_ref[0], preferred_element_type=F32)
        up = jnp.dot(x, wu_ref[0], preferred_element_type=F32)
        mid = (gate * jax.nn.sigmoid(gate) * up).astype(BF16)
        _write_token_tiles(y_ref, jnp.dot(mid, wd_ref[0], preferred_element_type=F32))

    @pl.when(i >= nused_ref[0])
    def _():
        y_ref[...] = jnp.zeros_like(y_ref)


def _experts(xb, block_e, n_used, w_gate, w_up, w_down):
    n_blocks = block_e.shape[0]
    rows = xb.shape[0] // (n_blocks * SUBLANES)
    D, F = w_gate.shape[1], w_gate.shape[2]
    tiles = pl.BlockSpec((rows * SUBLANES, LANES), lambda i, be, nu: (i, 0))
    return pl.pallas_call(
        functools.partial(_expert_kernel, rows=rows),
        grid_spec=pltpu.PrefetchScalarGridSpec(
            num_scalar_prefetch=2,
            grid=(n_blocks,),
            in_specs=[tiles,
                      pl.BlockSpec((1, D, F), lambda i, be, nu: (be[i], 0, 0)),
                      pl.BlockSpec((1, D, F), lambda i, be, nu: (be[i], 0, 0)),
                      pl.BlockSpec((1, F, D), lambda i, be, nu: (be[i], 0, 0))],
            out_specs=tiles,
        ),
        out_shape=jax.ShapeDtypeStruct(xb.shape, F32),
        compiler_params=_params("arbitrary"),
        name="moe_experts",
    )(block_e, n_used, xb, w_gate, w_up, w_down)


def _dispatch_plan(route, n_tokens):
    A = n_tokens * 2
    e_flat = route[:, :2].astype(jnp.int32).reshape(A)
    onehot = (e_flat[:, None] == jnp.arange(MOE_N_EXPERTS, dtype=jnp.int32)[None, :]).astype(jnp.int32)
    csum = jnp.cumsum(onehot, axis=0)
    rank = jnp.take_along_axis(csum, e_flat[:, None], axis=1)[:, 0] - 1
    counts = csum[-1]
    padded = ((counts + EXPERT_BLOCK - 1) // EXPERT_BLOCK) * EXPERT_BLOCK
    pends = jnp.cumsum(padded)
    pstarts = pends - padded
    dest = pstarts[e_flat] + rank
    n_blocks = -(-A // EXPERT_BLOCK) + MOE_N_EXPERTS
    block_start = jnp.arange(n_blocks, dtype=jnp.int32) * EXPERT_BLOCK
    block_e = jnp.minimum(jnp.searchsorted(pends, block_start, side='right'),
                          MOE_N_EXPERTS - 1).astype(jnp.int32)
    n_used = (pends[-1] // EXPERT_BLOCK).astype(jnp.int32).reshape(1)
    tok = jnp.arange(A, dtype=jnp.int32) // 2
    src_tok = jnp.zeros((n_blocks * EXPERT_BLOCK,), jnp.int32).at[dest].set(tok)
    return src_tok, dest.astype(jnp.int32), block_e, n_used


def _moe_dispatch(h, g, w_coarse, b_coarse, w_fine, b_fine, w_gate, w_up, w_down):
    T = h.shape[0]
    xn_tiles, route = _route(h, g, w_coarse, b_coarse, w_fine, b_fine)
    src_tok, dest, block_e, n_used = _dispatch_plan(route, T)
    xb = _gather_rows(xn_tiles.reshape(T, SUBLANES, LANES), src_tok)
    yb = _experts(xb.reshape(-1, LANES), block_e, n_used, w_gate, w_up, w_down)
    yc = _gather_rows(yb.reshape(-1, SUBLANES, LANES), dest)
    return yc.reshape(-1, LANES), route


def _ple_kernel(h_ref, yc_ref, route_ref, p_ref, g_ref, wg_ref, wp_ref, gf_ref, o_ref, *, tm, final):
    route = route_ref[...]
    y0 = _read_token_tiles(yc_ref, 0, 2 * SUBLANES, tm)
    y1 = _read_token_tiles(yc_ref, SUBLANES, 2 * SUBLANES, tm)
    h = h_ref[...] + route[:, 2:3] * y0 + route[:, 3:4] * y1
    xn = _rms(h, g_ref[...]).astype(BF16)
    gate = jax.nn.sigmoid(jnp.dot(xn, wg_ref[...], preferred_element_type=F32))
    proj = jnp.dot(p_ref[...].astype(BF16), wp_ref[...], preferred_element_type=F32)
    h = h + gate * proj
    if final:
        h = _rms(h, gf_ref[...])
    o_ref[...] = h


def _combine_ple(h, yc, route, p, g, w_gate, w_proj, g_final, final):
    T, D = h.shape
    tm = min(ROW_TILE, T)
    wg, wp = w_gate.astype(BF16), w_proj.astype(BF16)
    g2, gf = g.reshape(1, D), g_final.reshape(1, D)
    full = lambda a: pl.BlockSpec(a.shape, lambda i: (0,) * a.ndim)
    rows = lambda n: pl.BlockSpec((tm, n), lambda i: (i, 0))
    return pl.pallas_call(
        functools.partial(_ple_kernel, tm=tm, final=final),
        grid=(T // tm,),
        in_specs=[rows(D), pl.BlockSpec((tm * 2 * SUBLANES, LANES), lambda i: (i, 0)), rows(LANES),
                  rows(p.shape[1]), full(g2), full(wg), full(wp), full(gf)],
        out_specs=rows(D),
        out_shape=jax.ShapeDtypeStruct((T, D), F32),
        compiler_params=_params("parallel"),
        name="combine_ple",
    )(h, yc, route, p, g2, wg, wp, gf)


def kernel(x, p, norm_mix, norm_ffn, norm_ple, norm_final, gla_w_in, gla_w_gk, gla_b_gk, gla_norm, gla_w_out, pool_w, pool_b, pool_scale, moe_w_coarse, moe_b_coarse, moe_w_fine, moe_b_fine, moe_w_gate, moe_w_up, moe_w_down, ple_w_gate, ple_w_proj):
    batch, seq, d_model = x.shape
    depth = p.shape[0]
    T = batch * seq
    h = x.reshape(T, d_model)
    for i in range(depth):
        j = i // 2
        if i % 2 == 0:
            h = _gla_layer(h, norm_mix[i], gla_w_in[j], gla_w_gk[j], gla_b_gk[j], gla_norm[j], gla_w_out[j],
                           batch, seq)
        else:
            h = _pool_layer(h, norm_mix[i], pool_w[j], pool_b[j].reshape(-1), pool_scale[j], batch, seq)
        yc, route = _moe_dispatch(h, norm_ffn[i], moe_w_coarse[i], moe_b_coarse[i], moe_w_fine[i], moe_b_fine[i],
                                  moe_w_gate[i].astype(BF16), moe_w_up[i].astype(BF16),
                                  moe_w_down[i].astype(BF16))
        h = _combine_ple(h, yc, route, p[i].reshape(T, -1), norm_ple[i], ple_w_gate[i], ple_w_proj[i],
                         norm_final, final=(i == depth - 1))
    return h.reshape(batch, seq, d_model)
```

```python
import functools

import jax
import jax.numpy as jnp
from jax import lax
from jax.experimental import pallas as pl
from jax.experimental.pallas import tpu as pltpu
from jax.experimental.pallas import tpu_sc as plsc

EPS = 1e-6
GLA_HEADS = 4
GLA_GATE_RANK = 16
GLA_GATE_NORMALIZER = 16.0
GLA_CHUNK = 64
POOL_WINDOWS = (2, 4, 8, 16)
MOE_GROUPS = 8
MOE_EXPERTS_PER_GROUP = 8
MOE_N_EXPERTS = MOE_GROUPS * MOE_EXPERTS_PER_GROUP

LANES = 128
SUBLANES = 8
VMEM_LIMIT = 56 * 1024 * 1024
ROW_TILE = 512
GLA_BLOCK = 256
EXPERT_BLOCK = 256
GATHER_CHUNK = 32
POOL_HALO = 16

BF16 = jnp.bfloat16
F32 = jnp.float32


def _params(*sem):
    return pltpu.CompilerParams(dimension_semantics=sem, vmem_limit_bytes=VMEM_LIMIT)


def _rms(x, g):
    return x * lax.rsqrt(jnp.mean(x * x, axis=-1, keepdims=True) + EPS) * g


def _read_token_tiles(ref, first, stride, rows):
    return jnp.concatenate(
        [ref[pl.ds(first + j, rows, stride=stride), :] for j in range(SUBLANES)], axis=-1)


def _write_token_tiles(ref, val):
    rows = val.shape[0]
    for j in range(SUBLANES):
        ref[pl.ds(j, rows, stride=SUBLANES), :] = val[:, j * LANES:(j + 1) * LANES]


def _gla_proj_kernel(h_ref, g_ref, wq_ref, wk_ref, wv_ref, wr_ref, wlr_ref, wgk_ref, bgk_ref,
                     q_ref, k_ref, v_ref, r_ref, gk_ref, *, q_scale):
    xn = _rms(h_ref[...], g_ref[...]).astype(BF16)
    q_ref[...] = (jnp.dot(xn, wq_ref[...], preferred_element_type=F32) * q_scale).astype(BF16)
    k_ref[...] = jnp.dot(xn, wk_ref[...], preferred_element_type=F32).astype(BF16)
    v_ref[...] = jnp.dot(xn, wv_ref[...], preferred_element_type=F32).astype(BF16)
    r_ref[...] = jnp.dot(xn, wr_ref[...], preferred_element_type=F32).astype(BF16)
    lr = jnp.dot(xn, wlr_ref[...], preferred_element_type=F32).astype(BF16)
    z = jnp.dot(lr, wgk_ref[...], preferred_element_type=F32) + bgk_ref[...]
    gk_ref[...] = (jnp.minimum(z, 0.0) - jnp.log(1.0 + jnp.exp(-jnp.abs(z)))) * (1.0 / GLA_GATE_NORMALIZER)


def _gla_proj(h, g, w_in, w_gk, b_gk):
    T, D = h.shape
    K = w_gk.shape[1]
    V = (w_in.shape[1] - GLA_GATE_RANK - 2 * K) // 2
    wq = w_in[:, :K].astype(BF16)
    wk = w_in[:, K:2 * K].astype(BF16)
    wv = w_in[:, 2 * K:2 * K + V].astype(BF16)
    wr = w_in[:, 2 * K + V:2 * K + 2 * V].astype(BF16)
    wlr = jnp.pad(w_in[:, 2 * K + 2 * V:], ((0, 0), (0, LANES - GLA_GATE_RANK))).astype(BF16)
    wgk = jnp.pad(w_gk, ((0, LANES - GLA_GATE_RANK), (0, 0))).astype(BF16)
    tm = min(ROW_TILE, T)
    full = lambda a: pl.BlockSpec(a.shape, lambda i: (0,) * a.ndim)
    rows = lambda n: pl.BlockSpec((tm, n), lambda i: (i, 0))
    g2 = g.reshape(1, D)
    b2 = b_gk.reshape(1, K)
    return pl.pallas_call(
        functools.partial(_gla_proj_kernel, q_scale=float((K // GLA_HEADS) ** -0.5)),
        grid=(T // tm,),
        in_specs=[rows(D), full(g2), full(wq), full(wk), full(wv), full(wr), full(wlr), full(wgk), full(b2)],
        out_specs=[rows(K), rows(K), rows(V), rows(V), rows(K)],
        out_shape=[jax.ShapeDtypeStruct((T, K), BF16), jax.ShapeDtypeStruct((T, K), BF16),
                   jax.ShapeDtypeStruct((T, V), BF16), jax.ShapeDtypeStruct((T, V), BF16),
                   jax.ShapeDtypeStruct((T, K), F32)],
        compiler_params=_params("parallel"),
        name="gla_proj",
    )(h, g2, wq, wk, wv, wr, wlr, wgk, b2)


def _gla_core_kernel(tri_ref, q_ref, k_ref, v_ref, r_ref, gk_ref, ng_ref, o_ref, state_ref, *, dk, dv):
    @pl.when(pl.program_id(1) == 0)
    def _():
        state_ref[...] = jnp.zeros_like(state_ref)

    C = GLA_CHUNK
    n_chunks = q_ref.shape[0] // C
    gk = gk_ref[...]
    gk_hi = gk.astype(BF16)
    gk_lo = (gk - gk_hi.astype(F32)).astype(BF16)
    tri = tri_ref[...]
    b_all = (jnp.dot(tri, gk_hi, preferred_element_type=F32)
             + jnp.dot(tri, gk_lo, preferred_element_type=F32))
    row = lax.broadcasted_iota(jnp.int32, (C, C), 0)
    col = lax.broadcasted_iota(jnp.int32, (C, C), 1)
    causal = col <= row
    ng = ng_ref[...]
    for hd in range(GLA_HEADS):
        ks = slice(hd * dk, (hd + 1) * dk)
        vs = slice(hd * dv, (hd + 1) * dv)
        for c in range(n_chunks):
            rs = slice(c * C, (c + 1) * C)
            b = b_all[rs, ks]
            b_last = b[C - 1:C, :]
            q = q_ref[rs, ks].astype(F32)
            k = k_ref[rs, ks].astype(F32)
            v = v_ref[rs, vs]
            q_dec = (q * jnp.exp(b)).astype(BF16)
            k_inv = (k * jnp.exp(-b)).astype(BF16)
            k_end = (k * jnp.exp(b_last - b)).astype(BF16)
            attn = lax.dot_general(q_dec, k_inv, (((1,), (1,)), ((), ())), preferred_element_type=F32)
            attn = jnp.where(causal, attn, 0.0).astype(BF16)
            st = state_ref[hd]
            o = jnp.dot(attn, v, preferred_element_type=F32)
            o = o + lax.dot_general(q_dec, st.astype(BF16), (((1,), (1,)), ((), ())),
                                    preferred_element_type=F32)
            upd = lax.dot_general(v, k_end, (((0,), (0,)), ((), ())), preferred_element_type=F32)
            state_ref[hd] = st * jnp.exp(b_last) + upd
            rr = r_ref[rs, vs].astype(F32)
            o_ref[rs, vs] = (_rms(o, ng) * (rr * jax.nn.sigmoid(rr))).astype(BF16)


def _gla_core(q, k, v, r, gk, norm_g, batch, seq):
    T, K = q.shape
    V = v.shape[1]
    dk, dv = K // GLA_HEADS, V // GLA_HEADS
    L = min(GLA_BLOCK, seq)
    nblk = seq // L
    idx = jnp.arange(L)
    tri = ((idx[None, :] <= idx[:, None]) & (idx[None, :] // GLA_CHUNK == idx[:, None] // GLA_CHUNK)).astype(BF16)
    ng = norm_g.reshape(1, dv)
    rows = lambda n: pl.BlockSpec((L, n), lambda b, s: (b * nblk + s, 0))
    full = lambda a: pl.BlockSpec(a.shape, lambda b, s: (0,) * a.ndim)
    return pl.pallas_call(
        functools.partial(_gla_core_kernel, dk=dk, dv=dv),
        grid=(batch, nblk),
        in_specs=[full(tri), rows(K), rows(K), rows(V), rows(V), rows(K), full(ng)],
        out_specs=rows(V),
        out_shape=jax.ShapeDtypeStruct((T, V), BF16),
        scratch_shapes=[pltpu.VMEM((GLA_HEADS, dv, dk), F32)],
        compiler_params=_params("parallel", "arbitrary"),
        name="gla_core",
    )(tri, q, k, v, r, gk, ng)


def _proj_residual_kernel(h_ref, x_ref, w_ref, o_ref):
    o_ref[...] = h_ref[...] + jnp.dot(x_ref[...], w_ref[...], preferred_element_type=F32)


def _proj_residual(h, x, w):
    T, D = h.shape
    tm = min(ROW_TILE, T)
    wb = w.astype(BF16)
    return pl.pallas_call(
        _proj_residual_kernel,
        grid=(T // tm,),
        in_specs=[pl.BlockSpec((tm, D), lambda i: (i, 0)), pl.BlockSpec((tm, x.shape[1]), lambda i: (i, 0)),
                  pl.BlockSpec(wb.shape, lambda i: (0, 0))],
        out_specs=pl.BlockSpec((tm, D), lambda i: (i, 0)),
        out_shape=jax.ShapeDtypeStruct((T, D), F32),
        compiler_params=_params("parallel"),
        name="gla_out_proj",
    )(h, x, wb)


def _gla_layer(h, g, w_in, w_gk, b_gk, norm_g, w_out, batch, seq):
    q, k, v, r, gk = _gla_proj(h, g, w_in, w_gk, b_gk)
    og = _gla_core(q, k, v, r, gk, norm_g, batch, seq)
    return _proj_residual(h, og, w_out)


def _pool_kernel(h_ref, g_ref, w_ref, b_ref, sc_ref, o_ref, carry_ref, *, ts):
    s = pl.program_id(1)

    @pl.when(s == 0)
    def _():
        carry_ref[...] = jnp.zeros_like(carry_ref)

    h = h_ref[...]
    xn = _rms(h, g_ref[...])
    ext = jnp.concatenate([carry_ref[...], xn], axis=0)
    carry_ref[...] = xn[ts - POOL_HALO:, :]
    gd = w_ref.shape[1]
    t = s * ts + lax.broadcasted_iota(jnp.int32, (ts, gd), 0)
    for gi, win in enumerate(POOL_WINDOWS):
        cols = slice(gi * gd, (gi + 1) * gd)
        acc = ext[:, cols]
        d = 1
        while d < win:
            acc = acc[d:, :] + acc[:-d, :]
            d *= 2
        start = POOL_HALO - (win - 1)
        wsum = acc[start:start + ts, :]
        cnt = jnp.minimum(t + 1, win).astype(F32)
        pooled = (wsum / cnt - xn[:, cols]).astype(BF16)
        y = jnp.dot(pooled, w_ref[gi], preferred_element_type=F32) + b_ref[:, cols]
        o_ref[:, cols] = h[:, cols] + y * sc_ref[:, cols]


def _pool_layer(h, g, w, b, scale, batch, seq):
    T, D = h.shape
    ts = min(ROW_TILE, seq)
    nblk = seq // ts
    wb = w.astype(BF16)
    g2, b2, s2 = g.reshape(1, D), b.reshape(1, D), scale.reshape(1, D)
    rows = pl.BlockSpec((ts, D), lambda bb, s: (bb * nblk + s, 0))
    full = lambda a: pl.BlockSpec(a.shape, lambda bb, s: (0,) * a.ndim)
    return pl.pallas_call(
        functools.partial(_pool_kernel, ts=ts),
        grid=(batch, nblk),
        in_specs=[rows, full(g2), full(wb), full(b2), full(s2)],
        out_specs=rows,
        out_shape=jax.ShapeDtypeStruct((T, D), F32),
        scratch_shapes=[pltpu.VMEM((POOL_HALO, D), F32)],
        compiler_params=_params("parallel", "arbitrary"),
        name="pool_mixer",
    )(h, g2, wb, b2, s2)


def _route_kernel(h_ref, g_ref, w_ref, b_ref, xn_ref, route_ref):
    xn = _rms(h_ref[...], g_ref[...])
    _write_token_tiles(xn_ref, xn)
    logits = jnp.dot(xn, w_ref[...], preferred_element_type=F32, precision=lax.Precision.HIGHEST) + b_ref[...]
    lane = lax.broadcasted_iota(jnp.int32, logits.shape, 1)
    neg = jnp.float32(-jnp.inf)
    big = jnp.int32(LANES)
    is_c = lane < MOE_GROUPS
    lc = jnp.where(is_c, logits, neg)
    cmax = jnp.max(lc, axis=-1, keepdims=True)
    g_top = jnp.min(jnp.where(lc == cmax, lane, big), axis=-1, keepdims=True)
    p_g = 1.0 / jnp.sum(jnp.where(is_c, jnp.exp(logits - cmax), 0.0), axis=-1, keepdims=True)
    fine_lane = lane - MOE_GROUPS
    in_grp = (fine_lane >= g_top * MOE_EXPERTS_PER_GROUP) & (fine_lane < (g_top + 1) * MOE_EXPERTS_PER_GROUP)
    lf = jnp.where(in_grp, logits, neg)
    m1 = jnp.max(lf, axis=-1, keepdims=True)
    i1 = jnp.min(jnp.where(lf == m1, lane, big), axis=-1, keepdims=True)
    lf2 = jnp.where(lane == i1, neg, lf)
    m2 = jnp.max(lf2, axis=-1, keepdims=True)
    i2 = jnp.min(jnp.where(lf2 == m2, lane, big), axis=-1, keepdims=True)
    e2 = jnp.exp(m2 - m1)
    w1 = p_g / (1.0 + e2)
    w2 = p_g * e2 / (1.0 + e2)
    out = jnp.where(lane == 0, (i1 - MOE_GROUPS).astype(F32),
                    jnp.where(lane == 1, (i2 - MOE_GROUPS).astype(F32),
                              jnp.where(lane == 2, w1, jnp.where(lane == 3, w2, 0.0))))
    route_ref[...] = out


def _route(h, g, w_coarse, b_coarse, w_fine, b_fine):
    T, D = h.shape
    n_log = MOE_GROUPS + MOE_N_EXPERTS
    wr = jnp.pad(jnp.concatenate([w_coarse, w_fine], axis=1), ((0, 0), (0, LANES - n_log)))
    br = jnp.pad(jnp.concatenate([b_coarse, b_fine]), (0, LANES - n_log)).reshape(1, LANES)
    g2 = g.reshape(1, D)
    tm = min(ROW_TILE, T)
    full = lambda a: pl.BlockSpec(a.shape, lambda i: (0,) * a.ndim)
    return pl.pallas_call(
        _route_kernel,
        grid=(T // tm,),
        in_specs=[pl.BlockSpec((tm, D), lambda i: (i, 0)), full(g2), full(wr), full(br)],
        out_specs=[pl.BlockSpec((tm * SUBLANES, LANES), lambda i: (i, 0)),
                   pl.BlockSpec((tm, LANES), lambda i: (i, 0))],
        out_shape=[jax.ShapeDtypeStruct((T * SUBLANES, LANES), F32), jax.ShapeDtypeStruct((T, LANES), F32)],
        compiler_params=_params("parallel"),
        name="moe_route",
    )(h, g2, wr, br)


def _gather_rows(src, idx):
    n = idx.shape[0]
    info = plsc.get_sparse_core_info()
    n_cores = info.num_cores
    n_workers = n_cores * info.num_subcores
    per_w = n // n_workers
    chunk = GATHER_CHUNK
    n_chunks = per_w // chunk
    assert n == per_w * n_workers and per_w == n_chunks * chunk and n_chunks % 2 == 0
    mesh = plsc.VectorSubcoreMesh(core_axis_name="core", subcore_axis_name="subcore")

    @pl.kernel(out_type=jax.ShapeDtypeStruct((n,) + src.shape[1:], src.dtype), mesh=mesh,
               scratch_types=[pltpu.VMEM((per_w,), jnp.int32),
                              pltpu.VMEM((2, chunk) + src.shape[1:], src.dtype),
                              pltpu.SemaphoreType.DMA((2,)),
                              pltpu.SemaphoreType.DMA((2,))],
               name="row_gather")
    def gather_kernel(src_hbm, idx_hbm, dst_hbm, idx_v, buf, read_sem, write_sem):
        base = (lax.axis_index("subcore") * n_cores + lax.axis_index("core")) * per_w
        pltpu.sync_copy(idx_hbm.at[pl.ds(base, per_w)], idx_v)

        def read(c, slot):
            return pltpu.make_async_copy(src_hbm.at[idx_v.at[pl.ds(c * chunk, chunk)]], buf.at[slot],
                                         read_sem.at[slot])

        def write(c, slot):
            return pltpu.make_async_copy(buf.at[slot], dst_hbm.at[pl.ds(base + c * chunk, chunk)],
                                         write_sem.at[slot])

        read(0, 0).start()

        @pl.loop(0, n_chunks, step=2)
        def _(c0):
            for slot in range(2):
                c = c0 + slot
                read(c, slot).wait()

                @pl.when(c >= 1)
                def _():
                    write(c - 1, 1 - slot).wait()

                @pl.when(c + 1 < n_chunks)
                def _():
                    read(c + 1, 1 - slot).start()

                write(c, slot).start()

        write(n_chunks - 1, 1).wait()

    return gather_kernel(src, idx)


def _expert_kernel(be_ref, nused_ref, x_ref, wg_ref, wu_ref, wd_ref, y_ref, *, rows):
    i = pl.program_id(0)

    @pl.when(i < nused_ref[0])
    def _():
        x = _read_token_tiles(x_ref, 0, SUBLANES, rows).astype(BF16)
        gate = jnp.dot(x, wg_ref[0], preferred_element_type=F32)
        up = jnp.dot(x, wu_ref[0], preferred_element_type=F32)
        mid = (gate * jax.nn.sigmoid(gate) * up).astype(BF16)
        _write_token_tiles(y_ref, jnp.dot(mid, wd_ref[0], preferred_element_type=F32))

    @pl.when(i >= nused_ref[0])
    def _():
        y_ref[...] = jnp.zeros_like(y_ref)


def _experts(xb, block_e, n_used, w_gate, w_up, w_down):
    n_blocks = block_e.shape[0]
    rows = xb.shape[0] // (n_blocks * SUBLANES)
    D, F = w_gate.shape[1], w_gate.shape[2]
    tiles = pl.BlockSpec((rows * SUBLANES, LANES), lambda i, be, nu: (i, 0))
    return pl.pallas_call(
        functools.partial(_expert_kernel, rows=rows),
        grid_spec=pltpu.PrefetchScalarGridSpec(
            num_scalar_prefetch=2,
            grid=(n_blocks,),
            in_specs=[tiles,
                      pl.BlockSpec((1, D, F), lambda i, be, nu: (be[i], 0, 0)),
                      pl.BlockSpec((1, D, F), lambda i, be, nu: (be[i], 0, 0)),
                      pl.BlockSpec((1, F, D), lambda i, be, nu: (be[i], 0, 0))],
            out_specs=tiles,
        ),
        out_shape=jax.ShapeDtypeStruct(xb.shape, F32),
        compiler_params=_params("arbitrary"),
        name="moe_experts",
    )(block_e, n_used, xb, w_gate, w_up, w_down)


def _dispatch_plan(route, n_tokens):
    A = n_tokens * 2
    e_flat = route[:, :2].astype(jnp.int32).reshape(A)
    onehot = (e_flat[:, None] == jnp.arange(MOE_N_EXPERTS, dtype=jnp.int32)[None, :]).astype(jnp.int32)
    csum = jnp.cumsum(onehot, axis=0)
    rank = jnp.take_along_axis(csum, e_flat[:, None], axis=1)[:, 0] - 1
    counts = csum[-1]
    padded = ((counts + EXPERT_BLOCK - 1) // EXPERT_BLOCK) * EXPERT_BLOCK
    pends = jnp.cumsum(padded)
    pstarts = pends - padded
    dest = pstarts[e_flat] + rank
    n_blocks = -(-A // EXPERT_BLOCK) + MOE_N_EXPERTS
    block_start = jnp.arange(n_blocks, dtype=jnp.int32) * EXPERT_BLOCK
    block_e = jnp.minimum(jnp.searchsorted(pends, block_start, side='right'),
                          MOE_N_EXPERTS - 1).astype(jnp.int32)
    n_used = (pends[-1] // EXPERT_BLOCK).astype(jnp.int32).reshape(1)
    tok = jnp.arange(A, dtype=jnp.int32) // 2
    n_rows = n_blocks * EXPERT_BLOCK
    src_tok = (jnp.arange(n_rows, dtype=jnp.int32) % n_tokens).at[dest].set(tok)
    return src_tok, dest.astype(jnp.int32), block_e, n_used


def _moe_dispatch(h, g, w_coarse, b_coarse, w_fine, b_fine, w_gate, w_up, w_down):
    T = h.shape[0]
    xn_tiles, route = _route(h, g, w_coarse, b_coarse, w_fine, b_fine)
    src_tok, dest, block_e, n_used = _dispatch_plan(route, T)
    xb = _gather_rows(xn_tiles.reshape(T, SUBLANES, LANES), src_tok)
    yb = _experts(xb.reshape(-1, LANES), block_e, n_used, w_gate, w_up, w_down)
    yc = _gather_rows(yb.reshape(-1, SUBLANES, LANES), dest)
    return yc.reshape(-1, LANES), route


def _ple_kernel(h_ref, yc_ref, route_ref, p_ref, g_ref, wg_ref, wp_ref, gf_ref, o_ref, *, tm, final):
    route = route_ref[...]
    y0 = _read_token_tiles(yc_ref, 0, 2 * SUBLANES, tm)
    y1 = _read_token_tiles(yc_ref, SUBLANES, 2 * SUBLANES, tm)
    h = h_ref[...] + route[:, 2:3] * y0 + route[:, 3:4] * y1
    xn = _rms(h, g_ref[...]).astype(BF16)
    gate = jax.nn.sigmoid(jnp.dot(xn, wg_ref[...], preferred_element_type=F32))
    proj = jnp.dot(p_ref[...].astype(BF16), wp_ref[...], preferred_element_type=F32)
    h = h + gate * proj
    if final:
        h = _rms(h, gf_ref[...])
    o_ref[...] = h


def _combine_ple(h, yc, route, p, g, w_gate, w_proj, g_final, final):
    T, D = h.shape
    tm = min(ROW_TILE, T)
    wg, wp = w_gate.astype(BF16), w_proj.astype(BF16)
    g2, gf = g.reshape(1, D), g_final.reshape(1, D)
    full = lambda a: pl.BlockSpec(a.shape, lambda i: (0,) * a.ndim)
    rows = lambda n: pl.BlockSpec((tm, n), lambda i: (i, 0))
    return pl.pallas_call(
        functools.partial(_ple_kernel, tm=tm, final=final),
        grid=(T // tm,),
        in_specs=[rows(D), pl.BlockSpec((tm * 2 * SUBLANES, LANES), lambda i: (i, 0)), rows(LANES),
                  rows(p.shape[1]), full(g2), full(wg), full(wp), full(gf)],
        out_specs=rows(D),
        out_shape=jax.ShapeDtypeStruct((T, D), F32),
        compiler_params=_params("parallel"),
        name="combine_ple",
    )(h, yc, route, p, g2, wg, wp, gf)


def kernel(x, p, norm_mix, norm_ffn, norm_ple, norm_final, gla_w_in, gla_w_gk, gla_b_gk, gla_norm, gla_w_out, pool_w, pool_b, pool_scale, moe_w_coarse, moe_b_coarse, moe_w_fine, moe_b_fine, moe_w_gate, moe_w_up, moe_w_down, ple_w_gate, ple_w_proj):
    batch, seq, d_model = x.shape
    depth = p.shape[0]
    T = batch * seq
    h = x.reshape(T, d_model)
    for i in range(depth):
        j = i // 2
        if i % 2 == 0:
            h = _gla_layer(h, norm_mix[i], gla_w_in[j], gla_w_gk[j], gla_b_gk[j], gla_norm[j], gla_w_out[j],
                           batch, seq)
        else:
            h = _pool_layer(h, norm_mix[i], pool_w[j], pool_b[j].reshape(-1), pool_scale[j], batch, seq)
        yc, route = _moe_dispatch(h, norm_ffn[i], moe_w_coarse[i], moe_b_coarse[i], moe_w_fine[i], moe_b_fine[i],
                                  moe_w_gate[i].astype(BF16), moe_w_up[i].astype(BF16),
                                  moe_w_down[i].astype(BF16))
        h = _combine_ple(h, yc, route, p[i].reshape(T, -1), norm_ple[i], ple_w_gate[i], ple_w_proj[i],
                         norm_final, final=(i == depth - 1))
    return h.reshape(batch, seq, d_model)
```

```python
import functools

import jax
import jax.numpy as jnp
from jax import lax
from jax.experimental import pallas as pl
from jax.experimental.pallas import tpu as pltpu
from jax.experimental.pallas import tpu_sc as plsc

EPS = 1e-6
GLA_HEADS = 4
GLA_GATE_RANK = 16
GLA_GATE_NORMALIZER = 16.0
GLA_CHUNK = 64
POOL_WINDOWS = (2, 4, 8, 16)
MOE_GROUPS = 8
MOE_EXPERTS_PER_GROUP = 8
MOE_N_EXPERTS = MOE_GROUPS * MOE_EXPERTS_PER_GROUP

LANES = 128
SUBLANES = 8
VMEM_LIMIT = 56 * 1024 * 1024
ROW_TILE = 512
GLA_BLOCK = 256
EXPERT_BLOCK = 256
GATHER_CHUNK = 32
POOL_HALO = 16

BF16 = jnp.bfloat16
F32 = jnp.float32


def _params(*sem):
    return pltpu.CompilerParams(dimension_semantics=sem, vmem_limit_bytes=VMEM_LIMIT)


def _rms(x, g):
    return x * lax.rsqrt(jnp.mean(x * x, axis=-1, keepdims=True) + EPS) * g


def _read_token_tiles(ref, first, stride, rows):
    return jnp.concatenate(
        [ref[pl.ds(first + j, rows, stride=stride), :] for j in range(SUBLANES)], axis=-1)


def _write_token_tiles(ref, val):
    rows = val.shape[0]
    for j in range(SUBLANES):
        ref[pl.ds(j, rows, stride=SUBLANES), :] = val[:, j * LANES:(j + 1) * LANES]


def _gla_proj_kernel(h_ref, g_ref, wq_ref, wk_ref, wv_ref, wr_ref, wlr_ref, wgk_ref, bgk_ref,
                     q_ref, k_ref, v_ref, r_ref, gk_ref, *, q_scale):
    xn = _rms(h_ref[...], g_ref[...]).astype(BF16)
    q_ref[...] = (jnp.dot(xn, wq_ref[...], preferred_element_type=F32) * q_scale).astype(BF16)
    k_ref[...] = jnp.dot(xn, wk_ref[...], preferred_element_type=F32).astype(BF16)
    v_ref[...] = jnp.dot(xn, wv_ref[...], preferred_element_type=F32).astype(BF16)
    r_ref[...] = jnp.dot(xn, wr_ref[...], preferred_element_type=F32).astype(BF16)
    lr = jnp.dot(xn, wlr_ref[...], preferred_element_type=F32).astype(BF16)
    z = jnp.dot(lr, wgk_ref[...], preferred_element_type=F32) + bgk_ref[...]
    gk_ref[...] = (jnp.minimum(z, 0.0) - jnp.log(1.0 + jnp.exp(-jnp.abs(z)))) * (1.0 / GLA_GATE_NORMALIZER)


def _gla_proj(h, g, w_in, w_gk, b_gk):
    T, D = h.shape
    K = w_gk.shape[1]
    V = (w_in.shape[1] - GLA_GATE_RANK - 2 * K) // 2
    wq = w_in[:, :K].astype(BF16)
    wk = w_in[:, K:2 * K].astype(BF16)
    wv = w_in[:, 2 * K:2 * K + V].astype(BF16)
    wr = w_in[:, 2 * K + V:2 * K + 2 * V].astype(BF16)
    wlr = jnp.pad(w_in[:, 2 * K + 2 * V:], ((0, 0), (0, LANES - GLA_GATE_RANK))).astype(BF16)
    wgk = jnp.pad(w_gk, ((0, LANES - GLA_GATE_RANK), (0, 0))).astype(BF16)
    tm = min(ROW_TILE, T)
    full = lambda a: pl.BlockSpec(a.shape, lambda i: (0,) * a.ndim)
    rows = lambda n: pl.BlockSpec((tm, n), lambda i: (i, 0))
    g2 = g.reshape(1, D)
    b2 = b_gk.reshape(1, K)
    return pl.pallas_call(
        functools.partial(_gla_proj_kernel, q_scale=float((K // GLA_HEADS) ** -0.5)),
        grid=(T // tm,),
        in_specs=[rows(D), full(g2), full(wq), full(wk), full(wv), full(wr), full(wlr), full(wgk), full(b2)],
        out_specs=[rows(K), rows(K), rows(V), rows(V), rows(K)],
        out_shape=[jax.ShapeDtypeStruct((T, K), BF16), jax.ShapeDtypeStruct((T, K), BF16),
                   jax.ShapeDtypeStruct((T, V), BF16), jax.ShapeDtypeStruct((T, V), BF16),
                   jax.ShapeDtypeStruct((T, K), F32)],
        compiler_params=_params("parallel"),
        name="gla_proj",
    )(h, g2, wq, wk, wv, wr, wlr, wgk, b2)


def _gla_core_kernel(tri_ref, q_ref, k_ref, v_ref, r_ref, gk_ref, ng_ref, o_ref, state_ref, *, dk, dv):
    @pl.when(pl.program_id(1) == 0)
    def _():
        state_ref[...] = jnp.zeros_like(state_ref)

    C = GLA_CHUNK
    n_chunks = q_ref.shape[0] // C
    gk = gk_ref[...]
    gk_hi = gk.astype(BF16)
    gk_lo = (gk - gk_hi.astype(F32)).astype(BF16)
    tri = tri_ref[...]
    b_all = (jnp.dot(tri, gk_hi, preferred_element_type=F32)
             + jnp.dot(tri, gk_lo, preferred_element_type=F32))
    row = lax.broadcasted_iota(jnp.int32, (C, C), 0)
    col = lax.broadcasted_iota(jnp.int32, (C, C), 1)
    causal = col <= row
    ng = ng_ref[...]
    for hd in range(GLA_HEADS):
        ks = slice(hd * dk, (hd + 1) * dk)
        vs = slice(hd * dv, (hd + 1) * dv)
        for c in range(n_chunks):
            rs = slice(c * C, (c + 1) * C)
            b = b_all[rs, ks]
            b_last = b[C - 1:C, :]
            q = q_ref[rs, ks].astype(F32)
            k = k_ref[rs, ks].astype(F32)
            v = v_ref[rs, vs]
            q_dec = (q * jnp.exp(b)).astype(BF16)
            k_inv = (k * jnp.exp(-b)).astype(BF16)
            k_end = (k * jnp.exp(b_last - b)).astype(BF16)
            attn = lax.dot_general(q_dec, k_inv, (((1,), (1,)), ((), ())), preferred_element_type=F32)
            attn = jnp.where(causal, attn, 0.0).astype(BF16)
            st = state_ref[hd]
            o = jnp.dot(attn, v, preferred_element_type=F32)
            o = o + lax.dot_general(q_dec, st.astype(BF16), (((1,), (1,)), ((), ())),
                                    preferred_element_type=F32)
            upd = lax.dot_general(v, k_end, (((0,), (0,)), ((), ())), preferred_element_type=F32)
            state_ref[hd] = st * jnp.exp(b_last) + upd
            rr = r_ref[rs, vs].astype(F32)
            o_ref[rs, vs] = (_rms(o, ng) * (rr * jax.nn.sigmoid(rr))).astype(BF16)


def _gla_core(q, k, v, r, gk, norm_g, batch, seq):
    T, K = q.shape
    V = v.shape[1]
    dk, dv = K // GLA_HEADS, V // GLA_HEADS
    L = min(GLA_BLOCK, seq)
    nblk = seq // L
    idx = jnp.arange(L)
    tri = ((idx[None, :] <= idx[:, None]) & (idx[None, :] // GLA_CHUNK == idx[:, None] // GLA_CHUNK)).astype(BF16)
    ng = norm_g.reshape(1, dv)
    rows = lambda n: pl.BlockSpec((L, n), lambda b, s: (b * nblk + s, 0))
    full = lambda a: pl.BlockSpec(a.shape, lambda b, s: (0,) * a.ndim)
    return pl.pallas_call(
        functools.partial(_gla_core_kernel, dk=dk, dv=dv),
        grid=(batch, nblk),
        in_specs=[full(tri), rows(K), rows(K), rows(V), rows(V), rows(K), full(ng)],
        out_specs=rows(V),
        out_shape=jax.ShapeDtypeStruct((T, V), BF16),
        scratch_shapes=[pltpu.VMEM((GLA_HEADS, dv, dk), F32)],
        compiler_params=_params("parallel", "arbitrary"),
        name="gla_core",
    )(tri, q, k, v, r, gk, ng)


def _proj_residual_kernel(h_ref, x_ref, w_ref, o_ref):
    o_ref[...] = h_ref[...] + jnp.dot(x_ref[...], w_ref[...], preferred_element_type=F32)


def _proj_residual(h, x, w):
    T, D = h.shape
    tm = min(ROW_TILE, T)
    wb = w.astype(BF16)
    return pl.pallas_call(
        _proj_residual_kernel,
        grid=(T // tm,),
        in_specs=[pl.BlockSpec((tm, D), lambda i: (i, 0)), pl.BlockSpec((tm, x.shape[1]), lambda i: (i, 0)),
                  pl.BlockSpec(wb.shape, lambda i: (0, 0))],
        out_specs=pl.BlockSpec((tm, D), lambda i: (i, 0)),
        out_shape=jax.ShapeDtypeStruct((T, D), F32),
        compiler_params=_params("parallel"),
        name="gla_out_proj",
    )(h, x, wb)


def _gla_layer(h, g, w_in, w_gk, b_gk, norm_g, w_out, batch, seq):
    q, k, v, r, gk = _gla_proj(h, g, w_in, w_gk, b_gk)
    og = _gla_core(q, k, v, r, gk, norm_g, batch, seq)
    return _proj_residual(h, og, w_out)


def _pool_kernel(h_ref, g_ref, w_ref, b_ref, sc_ref, o_ref, carry_ref, *, ts):
    s = pl.program_id(1)

    @pl.when(s == 0)
    def _():
        carry_ref[...] = jnp.zeros_like(carry_ref)

    h = h_ref[...]
    xn = _rms(h, g_ref[...])
    ext = jnp.concatenate([carry_ref[...], xn], axis=0)
    carry_ref[...] = xn[ts - POOL_HALO:, :]
    gd = w_ref.shape[1]
    t = s * ts + lax.broadcasted_iota(jnp.int32, (ts, gd), 0)
    for gi, win in enumerate(POOL_WINDOWS):
        cols = slice(gi * gd, (gi + 1) * gd)
        acc = ext[:, cols]
        d = 1
        while d < win:
            acc = acc[d:, :] + acc[:-d, :]
            d *= 2
        start = POOL_HALO - (win - 1)
        wsum = acc[start:start + ts, :]
        cnt = jnp.minimum(t + 1, win).astype(F32)
        pooled = (wsum / cnt - xn[:, cols]).astype(BF16)
        y = jnp.dot(pooled, w_ref[gi], preferred_element_type=F32) + b_ref[:, cols]
        o_ref[:, cols] = h[:, cols] + y * sc_ref[:, cols]


def _pool_layer(h, g, w, b, scale, batch, seq):
    T, D = h.shape
    ts = min(ROW_TILE, seq)
    nblk = seq // ts
    wb = w.astype(BF16)
    g2, b2, s2 = g.reshape(1, D), b.reshape(1, D), scale.reshape(1, D)
    rows = pl.BlockSpec((ts, D), lambda bb, s: (bb * nblk + s, 0))
    full = lambda a: pl.BlockSpec(a.shape, lambda bb, s: (0,) * a.ndim)
    return pl.pallas_call(
        functools.partial(_pool_kernel, ts=ts),
        grid=(batch, nblk),
        in_specs=[rows, full(g2), full(wb), full(b2), full(s2)],
        out_specs=rows,
        out_shape=jax.ShapeDtypeStruct((T, D), F32),
        scratch_shapes=[pltpu.VMEM((POOL_HALO, D), F32)],
        compiler_params=_params("parallel", "arbitrary"),
        name="pool_mixer",
    )(h, g2, wb, b2, s2)


def _route_kernel(h_ref, g_ref, w_ref, b_ref, ltri_ref, xn_ref, route_ref, count_ref):
    @pl.when(pl.program_id(0) == 0)
    def _():
        count_ref[...] = jnp.zeros_like(count_ref)

    xn = _rms(h_ref[...], g_ref[...])
    _write_token_tiles(xn_ref, xn)
    logits = jnp.dot(xn, w_ref[...], preferred_element_type=F32, precision=lax.Precision.HIGHEST) + b_ref[...]
    lane = lax.broadcasted_iota(jnp.int32, logits.shape, 1)
    neg = jnp.float32(-jnp.inf)
    big = jnp.int32(LANES)
    is_c = lane < MOE_GROUPS
    lc = jnp.where(is_c, logits, neg)
    cmax = jnp.max(lc, axis=-1, keepdims=True)
    g_top = jnp.min(jnp.where(lc == cmax, lane, big), axis=-1, keepdims=True)
    p_g = 1.0 / jnp.sum(jnp.where(is_c, jnp.exp(logits - cmax), 0.0), axis=-1, keepdims=True)
    fine_lane = lane - MOE_GROUPS
    in_grp = (fine_lane >= g_top * MOE_EXPERTS_PER_GROUP) & (fine_lane < (g_top + 1) * MOE_EXPERTS_PER_GROUP)
    lf = jnp.where(in_grp, logits, neg)
    m1 = jnp.max(lf, axis=-1, keepdims=True)
    i1 = jnp.min(jnp.where(lf == m1, lane, big), axis=-1, keepdims=True)
    lf2 = jnp.where(lane == i1, neg, lf)
    m2 = jnp.max(lf2, axis=-1, keepdims=True)
    i2 = jnp.min(jnp.where(lf2 == m2, lane, big), axis=-1, keepdims=True)
    e2 = jnp.exp(m2 - m1)
    w1 = p_g / (1.0 + e2)
    w2 = p_g * e2 / (1.0 + e2)
    hot1 = lane == i1
    hot2 = lane == i2
    ltri = ltri_ref[...]
    before1 = jnp.dot(ltri, hot1.astype(BF16), preferred_element_type=F32)
    before2 = jnp.dot(ltri, hot2.astype(BF16), preferred_element_type=F32)
    base = count_ref[...]
    total1 = jnp.sum(hot1.astype(F32), axis=0, keepdims=True)
    total2 = jnp.sum(hot2.astype(F32), axis=0, keepdims=True)
    r1 = jnp.sum(jnp.where(hot1, before1 + base, 0.0), axis=-1, keepdims=True)
    r2 = jnp.sum(jnp.where(hot2, before2 + (base + total1), 0.0), axis=-1, keepdims=True)
    count_ref[...] = base + total1 + total2
    out = jnp.where(lane == 0, (i1 - MOE_GROUPS).astype(F32),
                    jnp.where(lane == 1, (i2 - MOE_GROUPS).astype(F32),
                              jnp.where(lane == 2, w1,
                                        jnp.where(lane == 3, w2,
                                                  jnp.where(lane == 4, r1, jnp.where(lane == 5, r2, 0.0))))))
    route_ref[...] = out


def _route(h, g, w_coarse, b_coarse, w_fine, b_fine):
    T, D = h.shape
    n_log = MOE_GROUPS + MOE_N_EXPERTS
    wr = jnp.pad(jnp.concatenate([w_coarse, w_fine], axis=1), ((0, 0), (0, LANES - n_log)))
    br = jnp.pad(jnp.concatenate([b_coarse, b_fine]), (0, LANES - n_log)).reshape(1, LANES)
    g2 = g.reshape(1, D)
    tm = min(ROW_TILE, T)
    idx = jnp.arange(tm)
    ltri = (idx[None, :] < idx[:, None]).astype(BF16)
    full = lambda a: pl.BlockSpec(a.shape, lambda i: (0,) * a.ndim)
    return pl.pallas_call(
        _route_kernel,
        grid=(T // tm,),
        in_specs=[pl.BlockSpec((tm, D), lambda i: (i, 0)), full(g2), full(wr), full(br), full(ltri)],
        out_specs=[pl.BlockSpec((tm * SUBLANES, LANES), lambda i: (i, 0)),
                   pl.BlockSpec((tm, LANES), lambda i: (i, 0)),
                   pl.BlockSpec((1, LANES), lambda i: (0, 0))],
        out_shape=[jax.ShapeDtypeStruct((T * SUBLANES, LANES), F32), jax.ShapeDtypeStruct((T, LANES), F32),
                   jax.ShapeDtypeStruct((1, LANES), F32)],
        compiler_params=_params("arbitrary"),
        name="moe_route",
    )(h, g2, wr, br, ltri)


def _gather_rows(src, idx):
    n = idx.shape[0]
    info = plsc.get_sparse_core_info()
    n_cores = info.num_cores
    n_workers = n_cores * info.num_subcores
    per_w = n // n_workers
    chunk = GATHER_CHUNK
    n_chunks = per_w // chunk
    assert n == per_w * n_workers and per_w == n_chunks * chunk and n_chunks % 2 == 0
    mesh = plsc.VectorSubcoreMesh(core_axis_name="core", subcore_axis_name="subcore")

    @pl.kernel(out_type=jax.ShapeDtypeStruct((n,) + src.shape[1:], src.dtype), mesh=mesh,
               scratch_types=[pltpu.VMEM((per_w,), jnp.int32),
                              pltpu.VMEM((2, chunk) + src.shape[1:], src.dtype),
                              pltpu.SemaphoreType.DMA((2,)),
                              pltpu.SemaphoreType.DMA((2,))],
               name="row_gather")
    def gather_kernel(src_hbm, idx_hbm, dst_hbm, idx_v, buf, read_sem, write_sem):
        base = (lax.axis_index("subcore") * n_cores + lax.axis_index("core")) * per_w
        pltpu.sync_copy(idx_hbm.at[pl.ds(base, per_w)], idx_v)

        def read(c, slot):
            return pltpu.make_async_copy(src_hbm.at[idx_v.at[pl.ds(c * chunk, chunk)]], buf.at[slot],
                                         read_sem.at[slot])

        def write(c, slot):
            return pltpu.make_async_copy(buf.at[slot], dst_hbm.at[pl.ds(base + c * chunk, chunk)],
                                         write_sem.at[slot])

        read(0, 0).start()

        @pl.loop(0, n_chunks, step=2)
        def _(c0):
            for slot in range(2):
                c = c0 + slot
                read(c, slot).wait()

                @pl.when(c >= 1)
                def _():
                    write(c - 1, 1 - slot).wait()

                @pl.when(c + 1 < n_chunks)
                def _():
                    read(c + 1, 1 - slot).start()

                write(c, slot).start()

        write(n_chunks - 1, 1).wait()

    return gather_kernel(src, idx)


def _expert_kernel(be_ref, nused_ref, x_ref, wg_ref, wu_ref, wd_ref, y_ref, wg_s, wu_s, wd_s, *, rows):
    i = pl.program_id(0)
    used = i < nused_ref[0]

    @pl.when(used & ((i == 0) | (be_ref[i] != be_ref[jnp.maximum(i - 1, 0)])))
    def _():
        wg_s[...] = wg_ref[0, 0].astype(BF16)
        wu_s[...] = wu_ref[0, 0].astype(BF16)
        wd_s[...] = wd_ref[0, 0].astype(BF16)

    @pl.when(used)
    def _():
        x = _read_token_tiles(x_ref, 0, SUBLANES, rows).astype(BF16)
        gate = jnp.dot(x, wg_s[...], preferred_element_type=F32)
        up = jnp.dot(x, wu_s[...], preferred_element_type=F32)
        mid = (gate * jax.nn.sigmoid(gate) * up).astype(BF16)
        _write_token_tiles(y_ref, jnp.dot(mid, wd_s[...], preferred_element_type=F32))

    @pl.when(jnp.logical_not(used))
    def _():
        y_ref[...] = jnp.zeros_like(y_ref)


def _experts(xb, block_e, n_used, w_gate, w_up, w_down, layer):
    n_blocks = block_e.shape[0]
    rows = xb.shape[0] // (n_blocks * SUBLANES)
    D, F = w_gate.shape[2], w_gate.shape[3]
    tiles = pl.BlockSpec((rows * SUBLANES, LANES), lambda i, be, nu: (i, 0))
    return pl.pallas_call(
        functools.partial(_expert_kernel, rows=rows),
        grid_spec=pltpu.PrefetchScalarGridSpec(
            num_scalar_prefetch=2,
            grid=(n_blocks,),
            in_specs=[tiles,
                      pl.BlockSpec((1, 1, D, F), lambda i, be, nu: (layer, be[i], 0, 0)),
                      pl.BlockSpec((1, 1, D, F), lambda i, be, nu: (layer, be[i], 0, 0)),
                      pl.BlockSpec((1, 1, F, D), lambda i, be, nu: (layer, be[i], 0, 0))],
            out_specs=tiles,
            scratch_shapes=[pltpu.VMEM((D, F), BF16), pltpu.VMEM((D, F), BF16), pltpu.VMEM((F, D), BF16)],
        ),
        out_shape=jax.ShapeDtypeStruct(xb.shape, F32),
        compiler_params=_params("arbitrary"),
        name="moe_experts",
    )(block_e, n_used, xb, w_gate, w_up, w_down)


def _dispatch_plan(route, counts, n_tokens):
    A = n_tokens * 2
    counts = counts[0, MOE_GROUPS:MOE_GROUPS + MOE_N_EXPERTS].astype(jnp.int32)
    blocks = (counts + EXPERT_BLOCK - 1) // EXPERT_BLOCK
    block_ends = jnp.cumsum(blocks)
    row_starts = (block_ends - blocks) * EXPERT_BLOCK
    expert = route[:, 0:2].astype(jnp.int32)
    rank = route[:, 4:6].astype(jnp.int32)
    dest = (row_starts[expert] + rank).reshape(A)
    n_blocks = -(-A // EXPERT_BLOCK) + MOE_N_EXPERTS
    block_e = jnp.minimum(jnp.searchsorted(block_ends, jnp.arange(n_blocks, dtype=jnp.int32), side='right'),
                          MOE_N_EXPERTS - 1).astype(jnp.int32)
    n_used = block_ends[-1:].astype(jnp.int32)
    tok = jnp.arange(A, dtype=jnp.int32) // 2
    n_rows = n_blocks * EXPERT_BLOCK
    src_tok = (jnp.arange(n_rows, dtype=jnp.int32) % n_tokens).at[dest].set(tok)
    return src_tok, dest, block_e, n_used


def _moe_dispatch(h, g, w_coarse, b_coarse, w_fine, b_fine, w_gate, w_up, w_down, layer):
    T = h.shape[0]
    xn_tiles, route, counts = _route(h, g, w_coarse, b_coarse, w_fine, b_fine)
    src_tok, dest, block_e, n_used = _dispatch_plan(route, counts, T)
    xb = _gather_rows(xn_tiles.reshape(T, SUBLANES, LANES), src_tok)
    yb = _experts(xb.reshape(-1, LANES), block_e, n_used, w_gate, w_up, w_down, layer)
    yc = _gather_rows(yb.reshape(-1, SUBLANES, LANES), dest)
    return yc.reshape(-1, LANES), route


def _ple_kernel(h_ref, yc_ref, route_ref, p_ref, g_ref, wg_ref, wp_ref, gf_ref, o_ref, *, tm, final):
    route = route_ref[...]
    y0 = _read_token_tiles(yc_ref, 0, 2 * SUBLANES, tm)
    y1 = _read_token_tiles(yc_ref, SUBLANES, 2 * SUBLANES, tm)
    h = h_ref[...] + route[:, 2:3] * y0 + route[:, 3:4] * y1
    xn = _rms(h, g_ref[...]).astype(BF16)
    gate = jax.nn.sigmoid(jnp.dot(xn, wg_ref[...], preferred_element_type=F32))
    proj = jnp.dot(p_ref[0].astype(BF16), wp_ref[...], preferred_element_type=F32)
    h = h + gate * proj
    if final:
        h = _rms(h, gf_ref[...])
    o_ref[...] = h


def _combine_ple(h, yc, route, p, layer, g, w_gate, w_proj, g_final, final):
    T, D = h.shape
    tm = min(ROW_TILE, T)
    wg, wp = w_gate.astype(BF16), w_proj.astype(BF16)
    g2, gf = g.reshape(1, D), g_final.reshape(1, D)
    full = lambda a: pl.BlockSpec(a.shape, lambda i: (0,) * a.ndim)
    rows = lambda n: pl.BlockSpec((tm, n), lambda i: (i, 0))
    return pl.pallas_call(
        functools.partial(_ple_kernel, tm=tm, final=final),
        grid=(T // tm,),
        in_specs=[rows(D), pl.BlockSpec((tm * 2 * SUBLANES, LANES), lambda i: (i, 0)), rows(LANES),
                  pl.BlockSpec((1, tm, p.shape[2]), lambda i: (layer, i, 0)), full(g2), full(wg), full(wp),
                  full(gf)],
        out_specs=rows(D),
        out_shape=jax.ShapeDtypeStruct((T, D), F32),
        compiler_params=_params("parallel"),
        name="combine_ple",
    )(h, yc, route, p, g2, wg, wp, gf)


def kernel(x, p, norm_mix, norm_ffn, norm_ple, norm_final, gla_w_in, gla_w_gk, gla_b_gk, gla_norm, gla_w_out, pool_w, pool_b, pool_scale, moe_w_coarse, moe_b_coarse, moe_w_fine, moe_b_fine, moe_w_gate, moe_w_up, moe_w_down, ple_w_gate, ple_w_proj):
    batch, seq, d_model = x.shape
    depth = p.shape[0]
    T = batch * seq
    h = x.reshape(T, d_model)
    p3 = p.reshape(depth, T, -1)
    for i in range(depth):
        j = i // 2
        if i % 2 == 0:
            h = _gla_layer(h, norm_mix[i], gla_w_in[j], gla_w_gk[j], gla_b_gk[j], gla_norm[j], gla_w_out[j],
                           batch, seq)
        else:
            h = _pool_layer(h, norm_mix[i], pool_w[j], pool_b[j].reshape(-1), pool_scale[j], batch, seq)
        yc, route = _moe_dispatch(h, norm_ffn[i], moe_w_coarse[i], moe_b_coarse[i], moe_w_fine[i], moe_b_fine[i],
                                  moe_w_gate, moe_w_up, moe_w_down, i)
        h = _combine_ple(h, yc, route, p3, i, norm_ple[i], ple_w_gate[i], ple_w_proj[i],
                         norm_final, final=(i == depth - 1))
    return h.reshape(batch, seq, d_model)
```

```python
import functools

import jax
import jax.numpy as jnp
from jax import lax
from jax.experimental import pallas as pl
from jax.experimental.pallas import tpu as pltpu
from jax.experimental.pallas import tpu_sc as plsc

EPS = 1e-6
GLA_HEADS = 4
GLA_GATE_RANK = 16
GLA_GATE_NORMALIZER = 16.0
GLA_CHUNK = 64
POOL_WINDOWS = (2, 4, 8, 16)
MOE_GROUPS = 8
MOE_EXPERTS_PER_GROUP = 8
MOE_N_EXPERTS = MOE_GROUPS * MOE_EXPERTS_PER_GROUP

LANES = 128
SUBLANES = 8
VMEM_LIMIT = 56 * 1024 * 1024
ROW_TILE = 512
GLA_BLOCK = 256
EXPERT_BLOCK = 256
GATHER_CHUNK = 32
POOL_HALO = 16

BF16 = jnp.bfloat16
F32 = jnp.float32


def _params(*sem):
    return pltpu.CompilerParams(dimension_semantics=sem, vmem_limit_bytes=VMEM_LIMIT)


def _rms(x, g):
    return x * lax.rsqrt(jnp.mean(x * x, axis=-1, keepdims=True) + EPS) * g


def _read_token_tiles(ref, first, stride, rows):
    return jnp.concatenate(
        [ref[pl.ds(first + j, rows, stride=stride), :] for j in range(SUBLANES)], axis=-1)


def _write_token_tiles(ref, val):
    rows = val.shape[0]
    for j in range(SUBLANES):
        ref[pl.ds(j, rows, stride=SUBLANES), :] = val[:, j * LANES:(j + 1) * LANES]


def _gla_proj_kernel(h_ref, g_ref, wq_ref, wk_ref, wv_ref, wr_ref, wlr_ref, wgk_ref, bgk_ref,
                     q_ref, k_ref, v_ref, r_ref, gk_ref, *, q_scale):
    xn = _rms(h_ref[...], g_ref[...]).astype(BF16)
    q_ref[...] = (jnp.dot(xn, wq_ref[...], preferred_element_type=F32) * q_scale).astype(BF16)
    k_ref[...] = jnp.dot(xn, wk_ref[...], preferred_element_type=F32).astype(BF16)
    v_ref[...] = jnp.dot(xn, wv_ref[...], preferred_element_type=F32).astype(BF16)
    r_ref[...] = jnp.dot(xn, wr_ref[...], preferred_element_type=F32).astype(BF16)
    lr = jnp.dot(xn, wlr_ref[...], preferred_element_type=F32).astype(BF16)
    z = jnp.dot(lr, wgk_ref[...], preferred_element_type=F32) + bgk_ref[...]
    gk_ref[...] = (jnp.minimum(z, 0.0) - jnp.log(1.0 + jnp.exp(-jnp.abs(z)))) * (1.0 / GLA_GATE_NORMALIZER)


def _gla_proj(h, g, w_in, w_gk, b_gk):
    T, D = h.shape
    K = w_gk.shape[1]
    V = (w_in.shape[1] - GLA_GATE_RANK - 2 * K) // 2
    wq = w_in[:, :K].astype(BF16)
    wk = w_in[:, K:2 * K].astype(BF16)
    wv = w_in[:, 2 * K:2 * K + V].astype(BF16)
    wr = w_in[:, 2 * K + V:2 * K + 2 * V].astype(BF16)
    wlr = jnp.pad(w_in[:, 2 * K + 2 * V:], ((0, 0), (0, LANES - GLA_GATE_RANK))).astype(BF16)
    wgk = jnp.pad(w_gk, ((0, LANES - GLA_GATE_RANK), (0, 0))).astype(BF16)
    tm = min(ROW_TILE, T)
    full = lambda a: pl.BlockSpec(a.shape, lambda i: (0,) * a.ndim)
    rows = lambda n: pl.BlockSpec((tm, n), lambda i: (i, 0))
    g2 = g.reshape(1, D)
    b2 = b_gk.reshape(1, K)
    return pl.pallas_call(
        functools.partial(_gla_proj_kernel, q_scale=float((K // GLA_HEADS) ** -0.5)),
        grid=(T // tm,),
        in_specs=[rows(D), full(g2), full(wq), full(wk), full(wv), full(wr), full(wlr), full(wgk), full(b2)],
        out_specs=[rows(K), rows(K), rows(V), rows(V), rows(K)],
        out_shape=[jax.ShapeDtypeStruct((T, K), BF16), jax.ShapeDtypeStruct((T, K), BF16),
                   jax.ShapeDtypeStruct((T, V), BF16), jax.ShapeDtypeStruct((T, V), BF16),
                   jax.ShapeDtypeStruct((T, K), F32)],
        compiler_params=_params("parallel"),
        name="gla_proj",
    )(h, g2, wq, wk, wv, wr, wlr, wgk, b2)


def _gla_core_kernel(tri_ref, q_ref, k_ref, v_ref, r_ref, gk_ref, ng_ref, o_ref, state_ref, *, dk, dv):
    @pl.when(pl.program_id(1) == 0)
    def _():
        state_ref[...] = jnp.zeros_like(state_ref)

    C = GLA_CHUNK
    n_chunks = q_ref.shape[0] // C
    gk = gk_ref[...]
    gk_hi = gk.astype(BF16)
    gk_lo = (gk - gk_hi.astype(F32)).astype(BF16)
    tri = tri_ref[...]
    b_all = (jnp.dot(tri, gk_hi, preferred_element_type=F32)
             + jnp.dot(tri, gk_lo, preferred_element_type=F32))
    row = lax.broadcasted_iota(jnp.int32, (C, C), 0)
    col = lax.broadcasted_iota(jnp.int32, (C, C), 1)
    causal = col <= row
    ng = ng_ref[...]
    for hd in range(GLA_HEADS):
        ks = slice(hd * dk, (hd + 1) * dk)
        vs = slice(hd * dv, (hd + 1) * dv)
        for c in range(n_chunks):
            rs = slice(c * C, (c + 1) * C)
            b = b_all[rs, ks]
            b_last = b[C - 1:C, :]
            q = q_ref[rs, ks].astype(F32)
            k = k_ref[rs, ks].astype(F32)
            v = v_ref[rs, vs]
            q_dec = (q * jnp.exp(b)).astype(BF16)
            k_inv = (k * jnp.exp(-b)).astype(BF16)
            k_end = (k * jnp.exp(b_last - b)).astype(BF16)
            attn = lax.dot_general(q_dec, k_inv, (((1,), (1,)), ((), ())), preferred_element_type=F32)
            attn = jnp.where(causal, attn, 0.0).astype(BF16)
            st = state_ref[hd]
            o = jnp.dot(attn, v, preferred_element_type=F32)
            o = o + lax.dot_general(q_dec, st.astype(BF16), (((1,), (1,)), ((), ())),
                                    preferred_element_type=F32)
            upd = lax.dot_general(v, k_end, (((0,), (0,)), ((), ())), preferred_element_type=F32)
            state_ref[hd] = st * jnp.exp(b_last) + upd
            rr = r_ref[rs, vs].astype(F32)
            o_ref[rs, vs] = (_rms(o, ng) * (rr * jax.nn.sigmoid(rr))).astype(BF16)


def _gla_core(q, k, v, r, gk, norm_g, batch, seq):
    T, K = q.shape
    V = v.shape[1]
    dk, dv = K // GLA_HEADS, V // GLA_HEADS
    L = min(GLA_BLOCK, seq)
    nblk = seq // L
    idx = jnp.arange(L)
    tri = ((idx[None, :] <= idx[:, None]) & (idx[None, :] // GLA_CHUNK == idx[:, None] // GLA_CHUNK)).astype(BF16)
    ng = norm_g.reshape(1, dv)
    rows = lambda n: pl.BlockSpec((L, n), lambda b, s: (b * nblk + s, 0))
    full = lambda a: pl.BlockSpec(a.shape, lambda b, s: (0,) * a.ndim)
    return pl.pallas_call(
        functools.partial(_gla_core_kernel, dk=dk, dv=dv),
        grid=(batch, nblk),
        in_specs=[full(tri), rows(K), rows(K), rows(V), rows(V), rows(K), full(ng)],
        out_specs=rows(V),
        out_shape=jax.ShapeDtypeStruct((T, V), BF16),
        scratch_shapes=[pltpu.VMEM((GLA_HEADS, dv, dk), F32)],
        compiler_params=_params("parallel", "arbitrary"),
        name="gla_core",
    )(tri, q, k, v, r, gk, ng)


def _proj_residual_kernel(h_ref, x_ref, w_ref, o_ref):
    o_ref[...] = h_ref[...] + jnp.dot(x_ref[...], w_ref[...], preferred_element_type=F32)


def _proj_residual(h, x, w):
    T, D = h.shape
    tm = min(ROW_TILE, T)
    wb = w.astype(BF16)
    return pl.pallas_call(
        _proj_residual_kernel,
        grid=(T // tm,),
        in_specs=[pl.BlockSpec((tm, D), lambda i: (i, 0)), pl.BlockSpec((tm, x.shape[1]), lambda i: (i, 0)),
                  pl.BlockSpec(wb.shape, lambda i: (0, 0))],
        out_specs=pl.BlockSpec((tm, D), lambda i: (i, 0)),
        out_shape=jax.ShapeDtypeStruct((T, D), F32),
        compiler_params=_params("parallel"),
        name="gla_out_proj",
    )(h, x, wb)


def _gla_layer(h, g, w_in, w_gk, b_gk, norm_g, w_out, batch, seq):
    q, k, v, r, gk = _gla_proj(h, g, w_in, w_gk, b_gk)
    og = _gla_core(q, k, v, r, gk, norm_g, batch, seq)
    return _proj_residual(h, og, w_out)


def _pool_kernel(h_ref, g_ref, w_ref, b_ref, sc_ref, o_ref, carry_ref, *, ts):
    s = pl.program_id(1)

    @pl.when(s == 0)
    def _():
        carry_ref[...] = jnp.zeros_like(carry_ref)

    h = h_ref[...]
    xn = _rms(h, g_ref[...])
    ext = jnp.concatenate([carry_ref[...], xn], axis=0)
    carry_ref[...] = xn[ts - POOL_HALO:, :]
    gd = w_ref.shape[1]
    t = s * ts + lax.broadcasted_iota(jnp.int32, (ts, gd), 0)
    for gi, win in enumerate(POOL_WINDOWS):
        cols = slice(gi * gd, (gi + 1) * gd)
        acc = ext[:, cols]
        d = 1
        while d < win:
            acc = acc[d:, :] + acc[:-d, :]
            d *= 2
        start = POOL_HALO - (win - 1)
        wsum = acc[start:start + ts, :]
        cnt = jnp.minimum(t + 1, win).astype(F32)
        pooled = (wsum / cnt - xn[:, cols]).astype(BF16)
        y = jnp.dot(pooled, w_ref[gi], preferred_element_type=F32) + b_ref[:, cols]
        o_ref[:, cols] = h[:, cols] + y * sc_ref[:, cols]


def _pool_layer(h, g, w, b, scale, batch, seq):
    T, D = h.shape
    ts = min(ROW_TILE, seq)
    nblk = seq // ts
    wb = w.astype(BF16)
    g2, b2, s2 = g.reshape(1, D), b.reshape(1, D), scale.reshape(1, D)
    rows = pl.BlockSpec((ts, D), lambda bb, s: (bb * nblk + s, 0))
    full = lambda a: pl.BlockSpec(a.shape, lambda bb, s: (0,) * a.ndim)
    return pl.pallas_call(
        functools.partial(_pool_kernel, ts=ts),
        grid=(batch, nblk),
        in_specs=[rows, full(g2), full(wb), full(b2), full(s2)],
        out_specs=rows,
        out_shape=jax.ShapeDtypeStruct((T, D), F32),
        scratch_shapes=[pltpu.VMEM((POOL_HALO, D), F32)],
        compiler_params=_params("parallel", "arbitrary"),
        name="pool_mixer",
    )(h, g2, wb, b2, s2)


def _route_kernel(h_ref, g_ref, w_ref, b_ref, ltri_ref, xn_ref, route_ref, count_ref):
    @pl.when(pl.program_id(0) == 0)
    def _():
        count_ref[...] = jnp.zeros_like(count_ref)

    xn = _rms(h_ref[...], g_ref[...])
    _write_token_tiles(xn_ref, xn)
    logits = jnp.dot(xn, w_ref[...], preferred_element_type=F32, precision=lax.Precision.HIGHEST) + b_ref[...]
    lane = lax.broadcasted_iota(jnp.int32, logits.shape, 1)
    neg = jnp.float32(-jnp.inf)
    big = jnp.int32(LANES)
    is_c = lane < MOE_GROUPS
    lc = jnp.where(is_c, logits, neg)
    cmax = jnp.max(lc, axis=-1, keepdims=True)
    g_top = jnp.min(jnp.where(lc == cmax, lane, big), axis=-1, keepdims=True)
    p_g = 1.0 / jnp.sum(jnp.where(is_c, jnp.exp(logits - cmax), 0.0), axis=-1, keepdims=True)
    fine_lane = lane - MOE_GROUPS
    in_grp = (fine_lane >= g_top * MOE_EXPERTS_PER_GROUP) & (fine_lane < (g_top + 1) * MOE_EXPERTS_PER_GROUP)
    lf = jnp.where(in_grp, logits, neg)
    m1 = jnp.max(lf, axis=-1, keepdims=True)
    i1 = jnp.min(jnp.where(lf == m1, lane, big), axis=-1, keepdims=True)
    lf2 = jnp.where(lane == i1, neg, lf)
    m2 = jnp.max(lf2, axis=-1, keepdims=True)
    i2 = jnp.min(jnp.where(lf2 == m2, lane, big), axis=-1, keepdims=True)
    e2 = jnp.exp(m2 - m1)
    w1 = p_g / (1.0 + e2)
    w2 = p_g * e2 / (1.0 + e2)
    hot1 = lane == i1
    hot2 = lane == i2
    ltri = ltri_ref[...]
    before1 = jnp.dot(ltri, hot1.astype(BF16), preferred_element_type=F32)
    before2 = jnp.dot(ltri, hot2.astype(BF16), preferred_element_type=F32)
    base = count_ref[...]
    total1 = jnp.sum(hot1.astype(F32), axis=0, keepdims=True)
    total2 = jnp.sum(hot2.astype(F32), axis=0, keepdims=True)
    r1 = jnp.sum(jnp.where(hot1, before1 + base, 0.0), axis=-1, keepdims=True)
    r2 = jnp.sum(jnp.where(hot2, before2 + (base + total1), 0.0), axis=-1, keepdims=True)
    count_ref[...] = base + total1 + total2
    out = jnp.where(lane == 0, (i1 - MOE_GROUPS).astype(F32),
                    jnp.where(lane == 1, (i2 - MOE_GROUPS).astype(F32),
                              jnp.where(lane == 2, w1,
                                        jnp.where(lane == 3, w2,
                                                  jnp.where(lane == 4, r1, jnp.where(lane == 5, r2, 0.0))))))
    route_ref[...] = out


def _route(h, g, w_coarse, b_coarse, w_fine, b_fine):
    T, D = h.shape
    n_log = MOE_GROUPS + MOE_N_EXPERTS
    wr = jnp.pad(jnp.concatenate([w_coarse, w_fine], axis=1), ((0, 0), (0, LANES - n_log)))
    br = jnp.pad(jnp.concatenate([b_coarse, b_fine]), (0, LANES - n_log)).reshape(1, LANES)
    g2 = g.reshape(1, D)
    tm = min(ROW_TILE, T)
    idx = jnp.arange(tm)
    ltri = (idx[None, :] < idx[:, None]).astype(BF16)
    full = lambda a: pl.BlockSpec(a.shape, lambda i: (0,) * a.ndim)
    return pl.pallas_call(
        _route_kernel,
        grid=(T // tm,),
        in_specs=[pl.BlockSpec((tm, D), lambda i: (i, 0)), full(g2), full(wr), full(br), full(ltri)],
        out_specs=[pl.BlockSpec((tm * SUBLANES, LANES), lambda i: (i, 0)),
                   pl.BlockSpec((tm, LANES), lambda i: (i, 0)),
                   pl.BlockSpec((1, LANES), lambda i: (0, 0))],
        out_shape=[jax.ShapeDtypeStruct((T * SUBLANES, LANES), F32), jax.ShapeDtypeStruct((T, LANES), F32),
                   jax.ShapeDtypeStruct((1, LANES), F32)],
        compiler_params=_params("arbitrary"),
        name="moe_route",
    )(h, g2, wr, br, ltri)


def _gather_rows(src, idx):
    n = idx.shape[0]
    info = plsc.get_sparse_core_info()
    n_cores = info.num_cores
    n_workers = n_cores * info.num_subcores
    per_w = n // n_workers
    chunk = GATHER_CHUNK
    n_chunks = per_w // chunk
    assert n == per_w * n_workers and per_w == n_chunks * chunk and n_chunks % 2 == 0
    mesh = plsc.VectorSubcoreMesh(core_axis_name="core", subcore_axis_name="subcore")

    @pl.kernel(out_type=jax.ShapeDtypeStruct((n,) + src.shape[1:], src.dtype), mesh=mesh,
               scratch_types=[pltpu.VMEM((per_w,), jnp.int32),
                              pltpu.VMEM((2, chunk) + src.shape[1:], src.dtype),
                              pltpu.SemaphoreType.DMA((2,)),
                              pltpu.SemaphoreType.DMA((2,))],
               name="row_gather")
    def gather_kernel(src_hbm, idx_hbm, dst_hbm, idx_v, buf, read_sem, write_sem):
        base = (lax.axis_index("subcore") * n_cores + lax.axis_index("core")) * per_w
        pltpu.sync_copy(idx_hbm.at[pl.ds(base, per_w)], idx_v)

        def read(c, slot):
            return pltpu.make_async_copy(src_hbm.at[idx_v.at[pl.ds(c * chunk, chunk)]], buf.at[slot],
                                         read_sem.at[slot])

        def write(c, slot):
            return pltpu.make_async_copy(buf.at[slot], dst_hbm.at[pl.ds(base + c * chunk, chunk)],
                                         write_sem.at[slot])

        read(0, 0).start()

        @pl.loop(0, n_chunks, step=2)
        def _(c0):
            for slot in range(2):
                c = c0 + slot
                read(c, slot).wait()

                @pl.when(c >= 1)
                def _():
                    write(c - 1, 1 - slot).wait()

                @pl.when(c + 1 < n_chunks)
                def _():
                    read(c + 1, 1 - slot).start()

                write(c, slot).start()

        write(n_chunks - 1, 1).wait()

    return gather_kernel(src, idx)


def _expert_kernel(be_ref, nused_ref, x_ref, wg_ref, wu_ref, wd_ref, y_ref, wg_s, wu_s, wd_s, *, rows):
    i = pl.program_id(0)
    used = i < nused_ref[0]

    @pl.when(used & ((i == 0) | (be_ref[i] != be_ref[jnp.maximum(i - 1, 0)])))
    def _():
        wg_s[...] = wg_ref[0, 0].astype(BF16)
        wu_s[...] = wu_ref[0, 0].astype(BF16)
        wd_s[...] = wd_ref[0, 0].astype(BF16)

    @pl.when(used)
    def _():
        x = _read_token_tiles(x_ref, 0, SUBLANES, rows).astype(BF16)
        gate = jnp.dot(x, wg_s[...], preferred_element_type=F32)
        up = jnp.dot(x, wu_s[...], preferred_element_type=F32)
        mid = (gate * jax.nn.sigmoid(gate) * up).astype(BF16)
        _write_token_tiles(y_ref, jnp.dot(mid, wd_s[...], preferred_element_type=F32))

    @pl.when(jnp.logical_not(used))
    def _():
        y_ref[...] = jnp.zeros_like(y_ref)


def _experts(xb, block_e, n_used, w_gate, w_up, w_down, layer):
    n_blocks = block_e.shape[0]
    rows = xb.shape[0] // (n_blocks * SUBLANES)
    D, F = w_gate.shape[2], w_gate.shape[3]
    tiles = pl.BlockSpec((rows * SUBLANES, LANES), lambda i, be, nu: (i, 0))
    return pl.pallas_call(
        functools.partial(_expert_kernel, rows=rows),
        grid_spec=pltpu.PrefetchScalarGridSpec(
            num_scalar_prefetch=2,
            grid=(n_blocks,),
            in_specs=[tiles,
                      pl.BlockSpec((1, 1, D, F), lambda i, be, nu: (layer, be[i], 0, 0)),
                      pl.BlockSpec((1, 1, D, F), lambda i, be, nu: (layer, be[i], 0, 0)),
                      pl.BlockSpec((1, 1, F, D), lambda i, be, nu: (layer, be[i], 0, 0))],
            out_specs=tiles,
            scratch_shapes=[pltpu.VMEM((D, F), BF16), pltpu.VMEM((D, F), BF16), pltpu.VMEM((F, D), BF16)],
        ),
        out_shape=jax.ShapeDtypeStruct(xb.shape, F32),
        compiler_params=_params("arbitrary"),
        name="moe_experts",
    )(block_e, n_used, xb, w_gate, w_up, w_down)


def _dispatch_plan(route, counts, n_tokens):
    A = n_tokens * 2
    counts = counts[0, MOE_GROUPS:MOE_GROUPS + MOE_N_EXPERTS].astype(jnp.int32)
    blocks = (counts + EXPERT_BLOCK - 1) // EXPERT_BLOCK
    block_ends = jnp.cumsum(blocks)
    row_starts = (block_ends - blocks) * EXPERT_BLOCK
    expert = route[:, 0:2].astype(jnp.int32)
    rank = route[:, 4:6].astype(jnp.int32)
    ids = jnp.arange(MOE_N_EXPERTS, dtype=jnp.int32)
    start_of = jnp.sum(jnp.where(expert[:, :, None] == ids, row_starts, 0), axis=-1)
    dest = (start_of + rank).reshape(A)
    n_blocks = -(-A // EXPERT_BLOCK) + MOE_N_EXPERTS
    block_e = jnp.minimum(jnp.sum(block_ends[None, :] <= jnp.arange(n_blocks, dtype=jnp.int32)[:, None], axis=1),
                          MOE_N_EXPERTS - 1).astype(jnp.int32)
    n_used = block_ends[-1:].astype(jnp.int32)
    tok = jnp.arange(A, dtype=jnp.int32) // 2
    n_rows = n_blocks * EXPERT_BLOCK
    src_tok = (jnp.arange(n_rows, dtype=jnp.int32) % n_tokens).at[dest].set(tok)
    return src_tok, dest, block_e, n_used


def _moe_dispatch(h, g, w_coarse, b_coarse, w_fine, b_fine, w_gate, w_up, w_down, layer):
    T = h.shape[0]
    xn_tiles, route, counts = _route(h, g, w_coarse, b_coarse, w_fine, b_fine)
    src_tok, dest, block_e, n_used = _dispatch_plan(route, counts, T)
    xb = _gather_rows(xn_tiles.reshape(T, SUBLANES, LANES), src_tok)
    yb = _experts(xb.reshape(-1, LANES), block_e, n_used, w_gate, w_up, w_down, layer)
    yc = _gather_rows(yb.reshape(-1, SUBLANES, LANES), dest)
    return yc.reshape(-1, LANES), route


def _ple_kernel(h_ref, yc_ref, route_ref, p_ref, g_ref, wg_ref, wp_ref, gf_ref, o_ref, *, tm, final):
    route = route_ref[...]
    y0 = _read_token_tiles(yc_ref, 0, 2 * SUBLANES, tm)
    y1 = _read_token_tiles(yc_ref, SUBLANES, 2 * SUBLANES, tm)
    h = h_ref[...] + route[:, 2:3] * y0 + route[:, 3:4] * y1
    xn = _rms(h, g_ref[...]).astype(BF16)
    gate = jax.nn.sigmoid(jnp.dot(xn, wg_ref[...], preferred_element_type=F32))
    proj = jnp.dot(p_ref[0].astype(BF16), wp_ref[...], preferred_element_type=F32)
    h = h + gate * proj
    if final:
        h = _rms(h, gf_ref[...])
    o_ref[...] = h


def _combine_ple(h, yc, route, p, layer, g, w_gate, w_proj, g_final, final):
    T, D = h.shape
    tm = min(ROW_TILE, T)
    wg, wp = w_gate.astype(BF16), w_proj.astype(BF16)
    g2, gf = g.reshape(1, D), g_final.reshape(1, D)
    full = lambda a: pl.BlockSpec(a.shape, lambda i: (0,) * a.ndim)
    rows = lambda n: pl.BlockSpec((tm, n), lambda i: (i, 0))
    return pl.pallas_call(
        functools.partial(_ple_kernel, tm=tm, final=final),
        grid=(T // tm,),
        in_specs=[rows(D), pl.BlockSpec((tm * 2 * SUBLANES, LANES), lambda i: (i, 0)), rows(LANES),
                  pl.BlockSpec((1, tm, p.shape[2]), lambda i: (layer, i, 0)), full(g2), full(wg), full(wp),
                  full(gf)],
        out_specs=rows(D),
        out_shape=jax.ShapeDtypeStruct((T, D), F32),
        compiler_params=_params("parallel"),
        name="combine_ple",
    )(h, yc, route, p, g2, wg, wp, gf)


def kernel(x, p, norm_mix, norm_ffn, norm_ple, norm_final, gla_w_in, gla_w_gk, gla_b_gk, gla_norm, gla_w_out, pool_w, pool_b, pool_scale, moe_w_coarse, moe_b_coarse, moe_w_fine, moe_b_fine, moe_w_gate, moe_w_up, moe_w_down, ple_w_gate, ple_w_proj):
    batch, seq, d_model = x.shape
    depth = p.shape[0]
    T = batch * seq
    h = x.reshape(T, d_model)
    p3 = p.reshape(depth, T, -1)
    for i in range(depth):
        j = i // 2
        if i % 2 == 0:
            h = _gla_layer(h, norm_mix[i], gla_w_in[j], gla_w_gk[j], gla_b_gk[j], gla_norm[j], gla_w_out[j],
                           batch, seq)
        else:
            h = _pool_layer(h, norm_mix[i], pool_w[j], pool_b[j].reshape(-1), pool_scale[j], batch, seq)
        yc, route = _moe_dispatch(h, norm_ffn[i], moe_w_coarse[i], moe_b_coarse[i], moe_w_fine[i], moe_b_fine[i],
                                  moe_w_gate, moe_w_up, moe_w_down, i)
        h = _combine_ple(h, yc, route, p3, i, norm_ple[i], ple_w_gate[i], ple_w_proj[i],
                         norm_final, final=(i == depth - 1))
    return h.reshape(batch, seq, d_model)
```

```python
import functools

import jax
import jax.numpy as jnp
from jax import lax
from jax.experimental import pallas as pl
from jax.experimental.pallas import tpu as pltpu
from jax.experimental.pallas import tpu_sc as plsc

EPS = 1e-6
GLA_HEADS = 4
GLA_GATE_RANK = 16
GLA_GATE_NORMALIZER = 16.0
GLA_CHUNK = 64
POOL_WINDOWS = (2, 4, 8, 16)
MOE_GROUPS = 8
MOE_EXPERTS_PER_GROUP = 8
MOE_N_EXPERTS = MOE_GROUPS * MOE_EXPERTS_PER_GROUP

LANES = 128
SUBLANES = 8
TILE_ROWS = SUBLANES // 2
VMEM_LIMIT = 56 * 1024 * 1024
ROW_TILE = 512
GLA_BLOCK = 256
EXPERT_BLOCK = 256
GATHER_CHUNK = 64
POOL_HALO = 16

BF16 = jnp.bfloat16
F32 = jnp.float32


def _params(*sem):
    return pltpu.CompilerParams(dimension_semantics=sem, vmem_limit_bytes=VMEM_LIMIT)


def _rms(x, g):
    return x * lax.rsqrt(jnp.mean(x * x, axis=-1, keepdims=True) + EPS) * g


def _read_token_tiles(ref, first, stride, rows):
    chunks = []
    for m in range(TILE_ROWS):
        word = ref[pl.ds(first + m, rows, stride=stride), :]
        chunks.append(pltpu.bitcast(word << 16, F32))
        chunks.append(pltpu.bitcast(word & jnp.uint32(0xFFFF0000), F32))
    return jnp.concatenate(chunks, axis=-1)


def _write_token_tiles(ref, val):
    rows = val.shape[0]
    bits = lambda x: pltpu.bitcast(x.astype(BF16).astype(F32), jnp.uint32)
    for m in range(TILE_ROWS):
        lo = bits(val[:, (2 * m) * LANES:(2 * m + 1) * LANES])
        hi = bits(val[:, (2 * m + 1) * LANES:(2 * m + 2) * LANES])
        ref[pl.ds(m, rows, stride=TILE_ROWS), :] = (lo >> 16) | hi


def _gla_proj_kernel(h_ref, g_ref, wq_ref, wk_ref, wv_ref, wr_ref, wlr_ref, wgk_ref, bgk_ref,
                     q_ref, k_ref, v_ref, r_ref, gk_ref, *, q_scale):
    xn = _rms(h_ref[...], g_ref[...]).astype(BF16)
    q_ref[...] = (jnp.dot(xn, wq_ref[...], preferred_element_type=F32) * q_scale).astype(BF16)
    k_ref[...] = jnp.dot(xn, wk_ref[...], preferred_element_type=F32).astype(BF16)
    v_ref[...] = jnp.dot(xn, wv_ref[...], preferred_element_type=F32).astype(BF16)
    r_ref[...] = jnp.dot(xn, wr_ref[...], preferred_element_type=F32).astype(BF16)
    lr = jnp.dot(xn, wlr_ref[...], preferred_element_type=F32).astype(BF16)
    z = jnp.dot(lr, wgk_ref[...], preferred_element_type=F32) + bgk_ref[...]
    gk_ref[...] = (jnp.minimum(z, 0.0) - jnp.log(1.0 + jnp.exp(-jnp.abs(z)))) * (1.0 / GLA_GATE_NORMALIZER)


def _gla_proj(h, g, w_in, w_gk, b_gk):
    T, D = h.shape
    K = w_gk.shape[1]
    V = (w_in.shape[1] - GLA_GATE_RANK - 2 * K) // 2
    wq = w_in[:, :K].astype(BF16)
    wk = w_in[:, K:2 * K].astype(BF16)
    wv = w_in[:, 2 * K:2 * K + V].astype(BF16)
    wr = w_in[:, 2 * K + V:2 * K + 2 * V].astype(BF16)
    wlr = jnp.pad(w_in[:, 2 * K + 2 * V:], ((0, 0), (0, LANES - GLA_GATE_RANK))).astype(BF16)
    wgk = jnp.pad(w_gk, ((0, LANES - GLA_GATE_RANK), (0, 0))).astype(BF16)
    tm = min(ROW_TILE, T)
    full = lambda a: pl.BlockSpec(a.shape, lambda i: (0,) * a.ndim)
    rows = lambda n: pl.BlockSpec((tm, n), lambda i: (i, 0))
    g2 = g.reshape(1, D)
    b2 = b_gk.reshape(1, K)
    return pl.pallas_call(
        functools.partial(_gla_proj_kernel, q_scale=float((K // GLA_HEADS) ** -0.5)),
        grid=(T // tm,),
        in_specs=[rows(D), full(g2), full(wq), full(wk), full(wv), full(wr), full(wlr), full(wgk), full(b2)],
        out_specs=[rows(K), rows(K), rows(V), rows(V), rows(K)],
        out_shape=[jax.ShapeDtypeStruct((T, K), BF16), jax.ShapeDtypeStruct((T, K), BF16),
                   jax.ShapeDtypeStruct((T, V), BF16), jax.ShapeDtypeStruct((T, V), BF16),
                   jax.ShapeDtypeStruct((T, K), F32)],
        compiler_params=_params("parallel"),
        name="gla_proj",
    )(h, g2, wq, wk, wv, wr, wlr, wgk, b2)


def _gla_core_kernel(tri_ref, q_ref, k_ref, v_ref, r_ref, gk_ref, ng_ref, o_ref, state_ref, *, dk, dv):
    @pl.when(pl.program_id(1) == 0)
    def _():
        state_ref[...] = jnp.zeros_like(state_ref)

    C = GLA_CHUNK
    n_chunks = q_ref.shape[0] // C
    gk = gk_ref[...]
    gk_hi = gk.astype(BF16)
    gk_lo = (gk - gk_hi.astype(F32)).astype(BF16)
    tri = tri_ref[...]
    b_all = (jnp.dot(tri, gk_hi, preferred_element_type=F32)
             + jnp.dot(tri, gk_lo, preferred_element_type=F32))
    row = lax.broadcasted_iota(jnp.int32, (C, C), 0)
    col = lax.broadcasted_iota(jnp.int32, (C, C), 1)
    causal = col <= row
    ng = ng_ref[...]
    for hd in range(GLA_HEADS):
        ks = slice(hd * dk, (hd + 1) * dk)
        vs = slice(hd * dv, (hd + 1) * dv)
        for c in range(n_chunks):
            rs = slice(c * C, (c + 1) * C)
            b = b_all[rs, ks]
            b_last = b[C - 1:C, :]
            q = q_ref[rs, ks].astype(F32)
            k = k_ref[rs, ks].astype(F32)
            v = v_ref[rs, vs]
            q_dec = (q * jnp.exp(b)).astype(BF16)
            k_inv = (k * jnp.exp(-b)).astype(BF16)
            k_end = (k * jnp.exp(b_last - b)).astype(BF16)
            attn = lax.dot_general(q_dec, k_inv, (((1,), (1,)), ((), ())), preferred_element_type=F32)
            attn = jnp.where(causal, attn, 0.0).astype(BF16)
            st = state_ref[hd]
            o = jnp.dot(attn, v, preferred_element_type=F32)
            o = o + lax.dot_general(q_dec, st.astype(BF16), (((1,), (1,)), ((), ())),
                                    preferred_element_type=F32)
            upd = lax.dot_general(v, k_end, (((0,), (0,)), ((), ())), preferred_element_type=F32)
            state_ref[hd] = st * jnp.exp(b_last) + upd
            rr = r_ref[rs, vs].astype(F32)
            o_ref[rs, vs] = (_rms(o, ng) * (rr * jax.nn.sigmoid(rr))).astype(BF16)


def _gla_core(q, k, v, r, gk, norm_g, batch, seq):
    T, K = q.shape
    V = v.shape[1]
    dk, dv = K // GLA_HEADS, V // GLA_HEADS
    L = min(GLA_BLOCK, seq)
    nblk = seq // L
    idx = jnp.arange(L)
    tri = ((idx[None, :] <= idx[:, None]) & (idx[None, :] // GLA_CHUNK == idx[:, None] // GLA_CHUNK)).astype(BF16)
    ng = norm_g.reshape(1, dv)
    rows = lambda n: pl.BlockSpec((L, n), lambda b, s: (b * nblk + s, 0))
    full = lambda a: pl.BlockSpec(a.shape, lambda b, s: (0,) * a.ndim)
    return pl.pallas_call(
        functools.partial(_gla_core_kernel, dk=dk, dv=dv),
        grid=(batch, nblk),
        in_specs=[full(tri), rows(K), rows(K), rows(V), rows(V), rows(K), full(ng)],
        out_specs=rows(V),
        out_shape=jax.ShapeDtypeStruct((T, V), BF16),
        scratch_shapes=[pltpu.VMEM((GLA_HEADS, dv, dk), F32)],
        compiler_params=_params("parallel", "arbitrary"),
        name="gla_core",
    )(tri, q, k, v, r, gk, ng)


def _proj_residual_kernel(h_ref, x_ref, w_ref, o_ref):
    o_ref[...] = h_ref[...] + jnp.dot(x_ref[...], w_ref[...], preferred_element_type=F32)


def _proj_residual(h, x, w):
    T, D = h.shape
    tm = min(ROW_TILE, T)
    wb = w.astype(BF16)
    return pl.pallas_call(
        _proj_residual_kernel,
        grid=(T // tm,),
        in_specs=[pl.BlockSpec((tm, D), lambda i: (i, 0)), pl.BlockSpec((tm, x.shape[1]), lambda i: (i, 0)),
                  pl.BlockSpec(wb.shape, lambda i: (0, 0))],
        out_specs=pl.BlockSpec((tm, D), lambda i: (i, 0)),
        out_shape=jax.ShapeDtypeStruct((T, D), F32),
        compiler_params=_params("parallel"),
        name="gla_out_proj",
    )(h, x, wb)


def _gla_layer(h, g, w_in, w_gk, b_gk, norm_g, w_out, batch, seq):
    q, k, v, r, gk = _gla_proj(h, g, w_in, w_gk, b_gk)
    og = _gla_core(q, k, v, r, gk, norm_g, batch, seq)
    return _proj_residual(h, og, w_out)


def _pool_kernel(h_ref, g_ref, w_ref, b_ref, sc_ref, o_ref, carry_ref, *, ts):
    s = pl.program_id(1)

    @pl.when(s == 0)
    def _():
        carry_ref[...] = jnp.zeros_like(carry_ref)

    h = h_ref[...]
    xn = _rms(h, g_ref[...])
    ext = jnp.concatenate([carry_ref[...], xn], axis=0)
    carry_ref[...] = xn[ts - POOL_HALO:, :]
    gd = w_ref.shape[1]
    t = s * ts + lax.broadcasted_iota(jnp.int32, (ts, gd), 0)
    for gi, win in enumerate(POOL_WINDOWS):
        cols = slice(gi * gd, (gi + 1) * gd)
        acc = ext[:, cols]
        d = 1
        while d < win:
            acc = acc[d:, :] + acc[:-d, :]
            d *= 2
        start = POOL_HALO - (win - 1)
        wsum = acc[start:start + ts, :]
        cnt = jnp.minimum(t + 1, win).astype(F32)
        pooled = (wsum / cnt - xn[:, cols]).astype(BF16)
        y = jnp.dot(pooled, w_ref[gi], preferred_element_type=F32) + b_ref[:, cols]
        o_ref[:, cols] = h[:, cols] + y * sc_ref[:, cols]


def _pool_layer(h, g, w, b, scale, batch, seq):
    T, D = h.shape
    ts = min(ROW_TILE, seq)
    nblk = seq // ts
    wb = w.astype(BF16)
    g2, b2, s2 = g.reshape(1, D), b.reshape(1, D), scale.reshape(1, D)
    rows = pl.BlockSpec((ts, D), lambda bb, s: (bb * nblk + s, 0))
    full = lambda a: pl.BlockSpec(a.shape, lambda bb, s: (0,) * a.ndim)
    return pl.pallas_call(
        functools.partial(_pool_kernel, ts=ts),
        grid=(batch, nblk),
        in_specs=[rows, full(g2), full(wb), full(b2), full(s2)],
        out_specs=rows,
        out_shape=jax.ShapeDtypeStruct((T, D), F32),
        scratch_shapes=[pltpu.VMEM((POOL_HALO, D), F32)],
        compiler_params=_params("parallel", "arbitrary"),
        name="pool_mixer",
    )(h, g2, wb, b2, s2)


def _route_kernel(h_ref, g_ref, w_ref, b_ref, ltri_ref, xn_ref, route_ref, count_ref):
    @pl.when(pl.program_id(0) == 0)
    def _():
        count_ref[...] = jnp.zeros_like(count_ref)

    xn = _rms(h_ref[...], g_ref[...])
    _write_token_tiles(xn_ref, xn)
    logits = jnp.dot(xn, w_ref[...], preferred_element_type=F32, precision=lax.Precision.HIGHEST) + b_ref[...]
    lane = lax.broadcasted_iota(jnp.int32, logits.shape, 1)
    neg = jnp.float32(-jnp.inf)
    big = jnp.int32(LANES)
    is_c = lane < MOE_GROUPS
    lc = jnp.where(is_c, logits, neg)
    cmax = jnp.max(lc, axis=-1, keepdims=True)
    g_top = jnp.min(jnp.where(lc == cmax, lane, big), axis=-1, keepdims=True)
    p_g = 1.0 / jnp.sum(jnp.where(is_c, jnp.exp(logits - cmax), 0.0), axis=-1, keepdims=True)
    fine_lane = lane - MOE_GROUPS
    in_grp = (fine_lane >= g_top * MOE_EXPERTS_PER_GROUP) & (fine_lane < (g_top + 1) * MOE_EXPERTS_PER_GROUP)
    lf = jnp.where(in_grp, logits, neg)
    m1 = jnp.max(lf, axis=-1, keepdims=True)
    i1 = jnp.min(jnp.where(lf == m1, lane, big), axis=-1, keepdims=True)
    lf2 = jnp.where(lane == i1, neg, lf)
    m2 = jnp.max(lf2, axis=-1, keepdims=True)
    i2 = jnp.min(jnp.where(lf2 == m2, lane, big), axis=-1, keepdims=True)
    e2 = jnp.exp(m2 - m1)
    w1 = p_g / (1.0 + e2)
    w2 = p_g * e2 / (1.0 + e2)
    hot1 = lane == i1
    hot2 = lane == i2
    ltri = ltri_ref[...]
    before1 = jnp.dot(ltri, hot1.astype(BF16), preferred_element_type=F32)
    before2 = jnp.dot(ltri, hot2.astype(BF16), preferred_element_type=F32)
    base = count_ref[...]
    total1 = jnp.sum(hot1.astype(F32), axis=0, keepdims=True)
    total2 = jnp.sum(hot2.astype(F32), axis=0, keepdims=True)
    r1 = jnp.sum(jnp.where(hot1, before1 + base, 0.0), axis=-1, keepdims=True)
    r2 = jnp.sum(jnp.where(hot2, before2 + (base + total1), 0.0), axis=-1, keepdims=True)
    count_ref[...] = base + total1 + total2
    out = jnp.where(lane == 0, (i1 - MOE_GROUPS).astype(F32),
                    jnp.where(lane == 1, (i2 - MOE_GROUPS).astype(F32),
                              jnp.where(lane == 2, w1,
                                        jnp.where(lane == 3, w2,
                                                  jnp.where(lane == 4, r1, jnp.where(lane == 5, r2, 0.0))))))
    route_ref[...] = out


def _route(h, g, w_coarse, b_coarse, w_fine, b_fine):
    T, D = h.shape
    n_log = MOE_GROUPS + MOE_N_EXPERTS
    wr = jnp.pad(jnp.concatenate([w_coarse, w_fine], axis=1), ((0, 0), (0, LANES - n_log)))
    br = jnp.pad(jnp.concatenate([b_coarse, b_fine]), (0, LANES - n_log)).reshape(1, LANES)
    g2 = g.reshape(1, D)
    tm = min(ROW_TILE, T)
    idx = jnp.arange(tm)
    ltri = (idx[None, :] < idx[:, None]).astype(BF16)
    full = lambda a: pl.BlockSpec(a.shape, lambda i: (0,) * a.ndim)
    return pl.pallas_call(
        _route_kernel,
        grid=(T // tm,),
        in_specs=[pl.BlockSpec((tm, D), lambda i: (i, 0)), full(g2), full(wr), full(br), full(ltri)],
        out_specs=[pl.BlockSpec((tm * TILE_ROWS, LANES), lambda i: (i, 0)),
                   pl.BlockSpec((tm, LANES), lambda i: (i, 0)),
                   pl.BlockSpec((1, LANES), lambda i: (0, 0))],
        out_shape=[jax.ShapeDtypeStruct((T * TILE_ROWS, LANES), jnp.uint32), jax.ShapeDtypeStruct((T, LANES), F32),
                   jax.ShapeDtypeStruct((1, LANES), F32)],
        compiler_params=_params("arbitrary"),
        name="moe_route",
    )(h, g2, wr, br, ltri)


def _gather_rows(src, idx):
    n = idx.shape[0]
    info = plsc.get_sparse_core_info()
    n_cores = info.num_cores
    n_workers = n_cores * info.num_subcores
    per_w = n // n_workers
    chunk = GATHER_CHUNK
    n_chunks = per_w // chunk
    assert n == per_w * n_workers and per_w == n_chunks * chunk and n_chunks % 2 == 0
    mesh = plsc.VectorSubcoreMesh(core_axis_name="core", subcore_axis_name="subcore")

    @pl.kernel(out_type=jax.ShapeDtypeStruct((n,) + src.shape[1:], src.dtype), mesh=mesh,
               scratch_types=[pltpu.VMEM((per_w,), jnp.int32),
                              pltpu.VMEM((2, chunk) + src.shape[1:], src.dtype),
                              pltpu.SemaphoreType.DMA((2,)),
                              pltpu.SemaphoreType.DMA((2,))],
               name="row_gather")
    def gather_kernel(src_hbm, idx_hbm, dst_hbm, idx_v, buf, read_sem, write_sem):
        base = (lax.axis_index("subcore") * n_cores + lax.axis_index("core")) * per_w
        pltpu.sync_copy(idx_hbm.at[pl.ds(base, per_w)], idx_v)

        def read(c, slot):
            return pltpu.make_async_copy(src_hbm.at[idx_v.at[pl.ds(c * chunk, chunk)]], buf.at[slot],
                                         read_sem.at[slot])

        def write(c, slot):
            return pltpu.make_async_copy(buf.at[slot], dst_hbm.at[pl.ds(base + c * chunk, chunk)],
                                         write_sem.at[slot])

        read(0, 0).start()

        @pl.loop(0, n_chunks, step=2)
        def _(c0):
            for slot in range(2):
                c = c0 + slot
                read(c, slot).wait()

                @pl.when(c >= 1)
                def _():
                    write(c - 1, 1 - slot).wait()

                @pl.when(c + 1 < n_chunks)
                def _():
                    read(c + 1, 1 - slot).start()

                write(c, slot).start()

        write(n_chunks - 1, 1).wait()

    return gather_kernel(src, idx)


def _expert_kernel(be_ref, nused_ref, x_ref, wg_ref, wu_ref, wd_ref, y_ref, wg_s, wu_s, wd_s, *, rows):
    i = pl.program_id(0)
    used = i < nused_ref[0]

    @pl.when(used & ((i == 0) | (be_ref[i] != be_ref[jnp.maximum(i - 1, 0)])))
    def _():
        wg_s[...] = wg_ref[0, 0].astype(BF16)
        wu_s[...] = wu_ref[0, 0].astype(BF16)
        wd_s[...] = wd_ref[0, 0].astype(BF16)

    @pl.when(used)
    def _():
        x = _read_token_tiles(x_ref, 0, TILE_ROWS, rows).astype(BF16)
        gate = jnp.dot(x, wg_s[...], preferred_element_type=F32)
        up = jnp.dot(x, wu_s[...], preferred_element_type=F32)
        mid = (gate * jax.nn.sigmoid(gate) * up).astype(BF16)
        _write_token_tiles(y_ref, jnp.dot(mid, wd_s[...], preferred_element_type=F32))

    @pl.when(jnp.logical_not(used))
    def _():
        y_ref[...] = jnp.zeros_like(y_ref)


def _experts(xb, block_e, n_used, w_gate, w_up, w_down, layer):
    n_blocks = block_e.shape[0]
    rows = xb.shape[0] // (n_blocks * TILE_ROWS)
    D, F = w_gate.shape[2], w_gate.shape[3]
    tiles = pl.BlockSpec((rows * TILE_ROWS, LANES), lambda i, be, nu: (i, 0))
    return pl.pallas_call(
        functools.partial(_expert_kernel, rows=rows),
        grid_spec=pltpu.PrefetchScalarGridSpec(
            num_scalar_prefetch=2,
            grid=(n_blocks,),
            in_specs=[tiles,
                      pl.BlockSpec((1, 1, D, F), lambda i, be, nu: (layer, be[i], 0, 0)),
                      pl.BlockSpec((1, 1, D, F), lambda i, be, nu: (layer, be[i], 0, 0)),
                      pl.BlockSpec((1, 1, F, D), lambda i, be, nu: (layer, be[i], 0, 0))],
            out_specs=tiles,
            scratch_shapes=[pltpu.VMEM((D, F), BF16), pltpu.VMEM((D, F), BF16), pltpu.VMEM((F, D), BF16)],
        ),
        out_shape=jax.ShapeDtypeStruct(xb.shape, xb.dtype),
        compiler_params=_params("arbitrary"),
        name="moe_experts",
    )(block_e, n_used, xb, w_gate, w_up, w_down)


def _dispatch_plan(route, counts, n_tokens):
    A = n_tokens * 2
    counts = counts[0, MOE_GROUPS:MOE_GROUPS + MOE_N_EXPERTS].astype(jnp.int32)
    blocks = (counts + EXPERT_BLOCK - 1) // EXPERT_BLOCK
    block_ends = jnp.cumsum(blocks)
    row_starts = (block_ends - blocks) * EXPERT_BLOCK
    expert = route[:, 0:2].astype(jnp.int32)
    rank = route[:, 4:6].astype(jnp.int32)
    ids = jnp.arange(MOE_N_EXPERTS, dtype=jnp.int32)
    start_of = jnp.sum(jnp.where(expert[:, :, None] == ids, row_starts, 0), axis=-1)
    dest = (start_of + rank).reshape(A)
    n_blocks = -(-A // EXPERT_BLOCK) + MOE_N_EXPERTS
    block_e = jnp.minimum(jnp.sum(block_ends[None, :] <= jnp.arange(n_blocks, dtype=jnp.int32)[:, None], axis=1),
                          MOE_N_EXPERTS - 1).astype(jnp.int32)
    n_used = block_ends[-1:].astype(jnp.int32)
    tok = jnp.arange(A, dtype=jnp.int32) // 2
    n_rows = n_blocks * EXPERT_BLOCK
    src_tok = (jnp.arange(n_rows, dtype=jnp.int32) % n_tokens).at[dest].set(tok)
    return src_tok, dest, block_e, n_used


def _moe_dispatch(h, g, w_coarse, b_coarse, w_fine, b_fine, w_gate, w_up, w_down, layer):
    T = h.shape[0]
    xn_tiles, route, counts = _route(h, g, w_coarse, b_coarse, w_fine, b_fine)
    src_tok, dest, block_e, n_used = _dispatch_plan(route, counts, T)
    xb = _gather_rows(xn_tiles.reshape(T, TILE_ROWS, LANES), src_tok)
    yb = _experts(xb.reshape(-1, LANES), block_e, n_used, w_gate, w_up, w_down, layer)
    yc = _gather_rows(yb.reshape(-1, TILE_ROWS, LANES), dest)
    return yc.reshape(-1, LANES), route


def _ple_kernel(h_ref, yc_ref, route_ref, p_ref, g_ref, wg_ref, wp_ref, gf_ref, o_ref, *, tm, final):
    route = route_ref[...]
    y0 = _read_token_tiles(yc_ref, 0, 2 * TILE_ROWS, tm)
    y1 = _read_token_tiles(yc_ref, TILE_ROWS, 2 * TILE_ROWS, tm)
    h = h_ref[...] + route[:, 2:3] * y0 + route[:, 3:4] * y1
    xn = _rms(h, g_ref[...]).astype(BF16)
    gate = jax.nn.sigmoid(jnp.dot(xn, wg_ref[...], preferred_element_type=F32))
    proj = jnp.dot(p_ref[0].astype(BF16), wp_ref[...], preferred_element_type=F32)
    h = h + gate * proj
    if final:
        h = _rms(h, gf_ref[...])
    o_ref[...] = h


def _combine_ple(h, yc, route, p, layer, g, w_gate, w_proj, g_final, final):
    T, D = h.shape
    tm = min(ROW_TILE, T)
    wg, wp = w_gate.astype(BF16), w_proj.astype(BF16)
    g2, gf = g.reshape(1, D), g_final.reshape(1, D)
    full = lambda a: pl.BlockSpec(a.shape, lambda i: (0,) * a.ndim)
    rows = lambda n: pl.BlockSpec((tm, n), lambda i: (i, 0))
    return pl.pallas_call(
        functools.partial(_ple_kernel, tm=tm, final=final),
        grid=(T // tm,),
        in_specs=[rows(D), pl.BlockSpec((tm * 2 * TILE_ROWS, LANES), lambda i: (i, 0)), rows(LANES),
                  pl.BlockSpec((1, tm, p.shape[2]), lambda i: (layer, i, 0)), full(g2), full(wg), full(wp),
                  full(gf)],
        out_specs=rows(D),
        out_shape=jax.ShapeDtypeStruct((T, D), F32),
        compiler_params=_params("parallel"),
        name="combine_ple",
    )(h, yc, route, p, g2, wg, wp, gf)


def kernel(x, p, norm_mix, norm_ffn, norm_ple, norm_final, gla_w_in, gla_w_gk, gla_b_gk, gla_norm, gla_w_out, pool_w, pool_b, pool_scale, moe_w_coarse, moe_b_coarse, moe_w_fine, moe_b_fine, moe_w_gate, moe_w_up, moe_w_down, ple_w_gate, ple_w_proj):
    batch, seq, d_model = x.shape
    depth = p.shape[0]
    T = batch * seq
    h = x.reshape(T, d_model)
    p3 = p.reshape(depth, T, -1)
    for i in range(depth):
        j = i // 2
        if i % 2 == 0:
            h = _gla_layer(h, norm_mix[i], gla_w_in[j], gla_w_gk[j], gla_b_gk[j], gla_norm[j], gla_w_out[j],
                           batch, seq)
        else:
            h = _pool_layer(h, norm_mix[i], pool_w[j], pool_b[j].reshape(-1), pool_scale[j], batch, seq)
        yc, route = _moe_dispatch(h, norm_ffn[i], moe_w_coarse[i], moe_b_coarse[i], moe_w_fine[i], moe_b_fine[i],
                                  moe_w_gate, moe_w_up, moe_w_down, i)
        h = _combine_ple(h, yc, route, p3, i, norm_ple[i], ple_w_gate[i], ple_w_proj[i],
                         norm_final, final=(i == depth - 1))
    return h.reshape(batch, seq, d_model)
```

```python
import functools

import jax
import jax.numpy as jnp
from jax import lax
from jax.experimental import pallas as pl
from jax.experimental.pallas import tpu as pltpu
from jax.experimental.pallas import tpu_sc as plsc

EPS = 1e-6
GLA_HEADS = 4
GLA_GATE_RANK = 16
GLA_GATE_NORMALIZER = 16.0
GLA_CHUNK = 64
POOL_WINDOWS = (2, 4, 8, 16)
MOE_GROUPS = 8
MOE_EXPERTS_PER_GROUP = 8
MOE_N_EXPERTS = MOE_GROUPS * MOE_EXPERTS_PER_GROUP

LANES = 128
SUBLANES = 8
TILE_ROWS = SUBLANES // 2
VMEM_LIMIT = 56 * 1024 * 1024
ROW_TILE = 512
GLA_BLOCK = 256
EXPERT_BLOCK = 256
GATHER_CHUNK = 64
SCATTER_CHUNK = 128
POOL_HALO = 16

BF16 = jnp.bfloat16
F32 = jnp.float32


def _params(*sem):
    return pltpu.CompilerParams(dimension_semantics=sem, vmem_limit_bytes=VMEM_LIMIT)


def _rms(x, g):
    return x * lax.rsqrt(jnp.mean(x * x, axis=-1, keepdims=True) + EPS) * g


def _read_token_tiles(ref, first, stride, rows):
    chunks = []
    for m in range(TILE_ROWS):
        word = ref[pl.ds(first + m, rows, stride=stride), :]
        chunks.append(pltpu.bitcast(word << 16, F32))
        chunks.append(pltpu.bitcast(word & jnp.uint32(0xFFFF0000), F32))
    return jnp.concatenate(chunks, axis=-1)


def _write_token_tiles(ref, val):
    rows = val.shape[0]
    bits = lambda x: pltpu.bitcast(x.astype(BF16).astype(F32), jnp.uint32)
    for m in range(TILE_ROWS):
        lo = bits(val[:, (2 * m) * LANES:(2 * m + 1) * LANES])
        hi = bits(val[:, (2 * m + 1) * LANES:(2 * m + 2) * LANES])
        ref[pl.ds(m, rows, stride=TILE_ROWS), :] = (lo >> 16) | hi


def _gla_proj_kernel(h_ref, g_ref, wq_ref, wk_ref, wv_ref, wr_ref, wlr_ref, wgk_ref, bgk_ref,
                     q_ref, k_ref, v_ref, r_ref, gk_ref, *, q_scale):
    xn = _rms(h_ref[...], g_ref[...]).astype(BF16)
    q_ref[...] = (jnp.dot(xn, wq_ref[...], preferred_element_type=F32) * q_scale).astype(BF16)
    k_ref[...] = jnp.dot(xn, wk_ref[...], preferred_element_type=F32).astype(BF16)
    v_ref[...] = jnp.dot(xn, wv_ref[...], preferred_element_type=F32).astype(BF16)
    r_ref[...] = jnp.dot(xn, wr_ref[...], preferred_element_type=F32).astype(BF16)
    lr = jnp.dot(xn, wlr_ref[...], preferred_element_type=F32).astype(BF16)
    z = jnp.dot(lr, wgk_ref[...], preferred_element_type=F32) + bgk_ref[...]
    gk_ref[...] = (jnp.minimum(z, 0.0) - jnp.log(1.0 + jnp.exp(-jnp.abs(z)))) * (1.0 / GLA_GATE_NORMALIZER)


def _gla_proj(h, g, w_in, w_gk, b_gk):
    T, D = h.shape
    K = w_gk.shape[1]
    V = (w_in.shape[1] - GLA_GATE_RANK - 2 * K) // 2
    wq = w_in[:, :K].astype(BF16)
    wk = w_in[:, K:2 * K].astype(BF16)
    wv = w_in[:, 2 * K:2 * K + V].astype(BF16)
    wr = w_in[:, 2 * K + V:2 * K + 2 * V].astype(BF16)
    wlr = jnp.pad(w_in[:, 2 * K + 2 * V:], ((0, 0), (0, LANES - GLA_GATE_RANK))).astype(BF16)
    wgk = jnp.pad(w_gk, ((0, LANES - GLA_GATE_RANK), (0, 0))).astype(BF16)
    tm = min(ROW_TILE, T)
    full = lambda a: pl.BlockSpec(a.shape, lambda i: (0,) * a.ndim)
    rows = lambda n: pl.BlockSpec((tm, n), lambda i: (i, 0))
    g2 = g.reshape(1, D)
    b2 = b_gk.reshape(1, K)
    return pl.pallas_call(
        functools.partial(_gla_proj_kernel, q_scale=float((K // GLA_HEADS) ** -0.5)),
        grid=(T // tm,),
        in_specs=[rows(D), full(g2), full(wq), full(wk), full(wv), full(wr), full(wlr), full(wgk), full(b2)],
        out_specs=[rows(K), rows(K), rows(V), rows(V), rows(K)],
        out_shape=[jax.ShapeDtypeStruct((T, K), BF16), jax.ShapeDtypeStruct((T, K), BF16),
                   jax.ShapeDtypeStruct((T, V), BF16), jax.ShapeDtypeStruct((T, V), BF16),
                   jax.ShapeDtypeStruct((T, K), F32)],
        compiler_params=_params("parallel"),
        name="gla_proj",
    )(h, g2, wq, wk, wv, wr, wlr, wgk, b2)


def _gla_core_kernel(tri_ref, q_ref, k_ref, v_ref, r_ref, gk_ref, ng_ref, o_ref, state_ref, *, dk, dv):
    @pl.when(pl.program_id(1) == 0)
    def _():
        state_ref[...] = jnp.zeros_like(state_ref)

    C = GLA_CHUNK
    n_chunks = q_ref.shape[0] // C
    gk = gk_ref[...]
    gk_hi = gk.astype(BF16)
    gk_lo = (gk - gk_hi.astype(F32)).astype(BF16)
    tri = tri_ref[...]
    b_all = (jnp.dot(tri, gk_hi, preferred_element_type=F32)
             + jnp.dot(tri, gk_lo, preferred_element_type=F32))
    row = lax.broadcasted_iota(jnp.int32, (C, C), 0)
    col = lax.broadcasted_iota(jnp.int32, (C, C), 1)
    causal = col <= row
    ng = ng_ref[...]
    for hd in range(GLA_HEADS):
        ks = slice(hd * dk, (hd + 1) * dk)
        vs = slice(hd * dv, (hd + 1) * dv)
        for c in range(n_chunks):
            rs = slice(c * C, (c + 1) * C)
            b = b_all[rs, ks]
            b_last = b[C - 1:C, :]
            q = q_ref[rs, ks].astype(F32)
            k = k_ref[rs, ks].astype(F32)
            v = v_ref[rs, vs]
            q_dec = (q * jnp.exp(b)).astype(BF16)
            k_inv = (k * jnp.exp(-b)).astype(BF16)
            k_end = (k * jnp.exp(b_last - b)).astype(BF16)
            attn = lax.dot_general(q_dec, k_inv, (((1,), (1,)), ((), ())), preferred_element_type=F32)
            attn = jnp.where(causal, attn, 0.0).astype(BF16)
            st = state_ref[hd]
            o = jnp.dot(attn, v, preferred_element_type=F32)
            o = o + lax.dot_general(q_dec, st.astype(BF16), (((1,), (1,)), ((), ())),
                                    preferred_element_type=F32)
            upd = lax.dot_general(v, k_end, (((0,), (0,)), ((), ())), preferred_element_type=F32)
            state_ref[hd] = st * jnp.exp(b_last) + upd
            rr = r_ref[rs, vs].astype(F32)
            o_ref[rs, vs] = (_rms(o, ng) * (rr * jax.nn.sigmoid(rr))).astype(BF16)


def _gla_core(q, k, v, r, gk, norm_g, batch, seq):
    T, K = q.shape
    V = v.shape[1]
    dk, dv = K // GLA_HEADS, V // GLA_HEADS
    L = min(GLA_BLOCK, seq)
    nblk = seq // L
    idx = jnp.arange(L)
    tri = ((idx[None, :] <= idx[:, None]) & (idx[None, :] // GLA_CHUNK == idx[:, None] // GLA_CHUNK)).astype(BF16)
    ng = norm_g.reshape(1, dv)
    rows = lambda n: pl.BlockSpec((L, n), lambda b, s: (b * nblk + s, 0))
    full = lambda a: pl.BlockSpec(a.shape, lambda b, s: (0,) * a.ndim)
    return pl.pallas_call(
        functools.partial(_gla_core_kernel, dk=dk, dv=dv),
        grid=(batch, nblk),
        in_specs=[full(tri), rows(K), rows(K), rows(V), rows(V), rows(K), full(ng)],
        out_specs=rows(V),
        out_shape=jax.ShapeDtypeStruct((T, V), BF16),
        scratch_shapes=[pltpu.VMEM((GLA_HEADS, dv, dk), F32)],
        compiler_params=_params("parallel", "arbitrary"),
        name="gla_core",
    )(tri, q, k, v, r, gk, ng)


def _proj_residual_kernel(h_ref, x_ref, w_ref, o_ref):
    o_ref[...] = h_ref[...] + jnp.dot(x_ref[...], w_ref[...], preferred_element_type=F32)


def _proj_residual(h, x, w):
    T, D = h.shape
    tm = min(ROW_TILE, T)
    wb = w.astype(BF16)
    return pl.pallas_call(
        _proj_residual_kernel,
        grid=(T // tm,),
        in_specs=[pl.BlockSpec((tm, D), lambda i: (i, 0)), pl.BlockSpec((tm, x.shape[1]), lambda i: (i, 0)),
                  pl.BlockSpec(wb.shape, lambda i: (0, 0))],
        out_specs=pl.BlockSpec((tm, D), lambda i: (i, 0)),
        out_shape=jax.ShapeDtypeStruct((T, D), F32),
        compiler_params=_params("parallel"),
        name="gla_out_proj",
    )(h, x, wb)


def _gla_layer(h, g, w_in, w_gk, b_gk, norm_g, w_out, batch, seq):
    q, k, v, r, gk = _gla_proj(h, g, w_in, w_gk, b_gk)
    og = _gla_core(q, k, v, r, gk, norm_g, batch, seq)
    return _proj_residual(h, og, w_out)


def _pool_kernel(h_ref, g_ref, w_ref, b_ref, sc_ref, o_ref, carry_ref, *, ts):
    s = pl.program_id(1)

    @pl.when(s == 0)
    def _():
        carry_ref[...] = jnp.zeros_like(carry_ref)

    h = h_ref[...]
    xn = _rms(h, g_ref[...])
    ext = jnp.concatenate([carry_ref[...], xn], axis=0)
    carry_ref[...] = xn[ts - POOL_HALO:, :]
    gd = w_ref.shape[1]
    t = s * ts + lax.broadcasted_iota(jnp.int32, (ts, gd), 0)
    for gi, win in enumerate(POOL_WINDOWS):
        cols = slice(gi * gd, (gi + 1) * gd)
        acc = ext[:, cols]
        d = 1
        while d < win:
            acc = acc[d:, :] + acc[:-d, :]
            d *= 2
        start = POOL_HALO - (win - 1)
        wsum = acc[start:start + ts, :]
        cnt = jnp.minimum(t + 1, win).astype(F32)
        pooled = (wsum / cnt - xn[:, cols]).astype(BF16)
        y = jnp.dot(pooled, w_ref[gi], preferred_element_type=F32) + b_ref[:, cols]
        o_ref[:, cols] = h[:, cols] + y * sc_ref[:, cols]


def _pool_layer(h, g, w, b, scale, batch, seq):
    T, D = h.shape
    ts = min(ROW_TILE, seq)
    nblk = seq // ts
    wb = w.astype(BF16)
    g2, b2, s2 = g.reshape(1, D), b.reshape(1, D), scale.reshape(1, D)
    rows = pl.BlockSpec((ts, D), lambda bb, s: (bb * nblk + s, 0))
    full = lambda a: pl.BlockSpec(a.shape, lambda bb, s: (0,) * a.ndim)
    return pl.pallas_call(
        functools.partial(_pool_kernel, ts=ts),
        grid=(batch, nblk),
        in_specs=[rows, full(g2), full(wb), full(b2), full(s2)],
        out_specs=rows,
        out_shape=jax.ShapeDtypeStruct((T, D), F32),
        scratch_shapes=[pltpu.VMEM((POOL_HALO, D), F32)],
        compiler_params=_params("parallel", "arbitrary"),
        name="pool_mixer",
    )(h, g2, wb, b2, s2)


def _route_kernel(h_ref, g_ref, w_ref, b_ref, ltri_ref, xn_ref, route_ref, count_ref):
    @pl.when(pl.program_id(0) == 0)
    def _():
        count_ref[...] = jnp.zeros_like(count_ref)

    xn = _rms(h_ref[...], g_ref[...])
    _write_token_tiles(xn_ref, xn)
    logits = jnp.dot(xn, w_ref[...], preferred_element_type=F32, precision=lax.Precision.HIGHEST) + b_ref[...]
    lane = lax.broadcasted_iota(jnp.int32, logits.shape, 1)
    neg = jnp.float32(-jnp.inf)
    big = jnp.int32(LANES)
    is_c = lane < MOE_GROUPS
    lc = jnp.where(is_c, logits, neg)
    cmax = jnp.max(lc, axis=-1, keepdims=True)
    g_top = jnp.min(jnp.where(lc == cmax, lane, big), axis=-1, keepdims=True)
    p_g = 1.0 / jnp.sum(jnp.where(is_c, jnp.exp(logits - cmax), 0.0), axis=-1, keepdims=True)
    fine_lane = lane - MOE_GROUPS
    in_grp = (fine_lane >= g_top * MOE_EXPERTS_PER_GROUP) & (fine_lane < (g_top + 1) * MOE_EXPERTS_PER_GROUP)
    lf = jnp.where(in_grp, logits, neg)
    m1 = jnp.max(lf, axis=-1, keepdims=True)
    i1 = jnp.min(jnp.where(lf == m1, lane, big), axis=-1, keepdims=True)
    lf2 = jnp.where(lane == i1, neg, lf)
    m2 = jnp.max(lf2, axis=-1, keepdims=True)
    i2 = jnp.min(jnp.where(lf2 == m2, lane, big), axis=-1, keepdims=True)
    e2 = jnp.exp(m2 - m1)
    w1 = p_g / (1.0 + e2)
    w2 = p_g * e2 / (1.0 + e2)
    hot1 = lane == i1
    hot2 = lane == i2
    ltri = ltri_ref[...]
    before1 = jnp.dot(ltri, hot1.astype(BF16), preferred_element_type=F32)
    before2 = jnp.dot(ltri, hot2.astype(BF16), preferred_element_type=F32)
    base = count_ref[...]
    total1 = jnp.sum(hot1.astype(F32), axis=0, keepdims=True)
    total2 = jnp.sum(hot2.astype(F32), axis=0, keepdims=True)
    r1 = jnp.sum(jnp.where(hot1, before1 + base, 0.0), axis=-1, keepdims=True)
    r2 = jnp.sum(jnp.where(hot2, before2 + (base + total1), 0.0), axis=-1, keepdims=True)
    count_ref[...] = base + total1 + total2
    out = jnp.where(lane == 0, (i1 - MOE_GROUPS).astype(F32),
                    jnp.where(lane == 1, (i2 - MOE_GROUPS).astype(F32),
                              jnp.where(lane == 2, w1,
                                        jnp.where(lane == 3, w2,
                                                  jnp.where(lane == 4, r1, jnp.where(lane == 5, r2, 0.0))))))
    route_ref[...] = out


def _route(h, g, w_coarse, b_coarse, w_fine, b_fine):
    T, D = h.shape
    n_log = MOE_GROUPS + MOE_N_EXPERTS
    wr = jnp.pad(jnp.concatenate([w_coarse, w_fine], axis=1), ((0, 0), (0, LANES - n_log)))
    br = jnp.pad(jnp.concatenate([b_coarse, b_fine]), (0, LANES - n_log)).reshape(1, LANES)
    g2 = g.reshape(1, D)
    tm = min(ROW_TILE, T)
    idx = jnp.arange(tm)
    ltri = (idx[None, :] < idx[:, None]).astype(BF16)
    full = lambda a: pl.BlockSpec(a.shape, lambda i: (0,) * a.ndim)
    return pl.pallas_call(
        _route_kernel,
        grid=(T // tm,),
        in_specs=[pl.BlockSpec((tm, D), lambda i: (i, 0)), full(g2), full(wr), full(br), full(ltri)],
        out_specs=[pl.BlockSpec((tm * TILE_ROWS, LANES), lambda i: (i, 0)),
                   pl.BlockSpec((tm, LANES), lambda i: (i, 0)),
                   pl.BlockSpec((1, LANES), lambda i: (0, 0))],
        out_shape=[jax.ShapeDtypeStruct((T * TILE_ROWS, LANES), jnp.uint32), jax.ShapeDtypeStruct((T, LANES), F32),
                   jax.ShapeDtypeStruct((1, LANES), F32)],
        compiler_params=_params("arbitrary"),
        name="moe_route",
    )(h, g2, wr, br, ltri)


def _gather_rows(src, idx):
    n = idx.shape[0]
    info = plsc.get_sparse_core_info()
    n_cores = info.num_cores
    n_workers = n_cores * info.num_subcores
    per_w = n // n_workers
    chunk = GATHER_CHUNK
    n_chunks = per_w // chunk
    assert n == per_w * n_workers and per_w == n_chunks * chunk and n_chunks % 2 == 0
    mesh = plsc.VectorSubcoreMesh(core_axis_name="core", subcore_axis_name="subcore")

    @pl.kernel(out_type=jax.ShapeDtypeStruct((n,) + src.shape[1:], src.dtype), mesh=mesh,
               scratch_types=[pltpu.VMEM((per_w,), jnp.int32),
                              pltpu.VMEM((2, chunk) + src.shape[1:], src.dtype),
                              pltpu.SemaphoreType.DMA((2,)),
                              pltpu.SemaphoreType.DMA((2,))],
               name="row_gather")
    def gather_kernel(src_hbm, idx_hbm, dst_hbm, idx_v, buf, read_sem, write_sem):
        base = (lax.axis_index("subcore") * n_cores + lax.axis_index("core")) * per_w
        pltpu.sync_copy(idx_hbm.at[pl.ds(base, per_w)], idx_v)

        def read(c, slot):
            return pltpu.make_async_copy(src_hbm.at[idx_v.at[pl.ds(c * chunk, chunk)]], buf.at[slot],
                                         read_sem.at[slot])

        def write(c, slot):
            return pltpu.make_async_copy(buf.at[slot], dst_hbm.at[pl.ds(base + c * chunk, chunk)],
                                         write_sem.at[slot])

        read(0, 0).start()

        @pl.loop(0, n_chunks, step=2)
        def _(c0):
            for slot in range(2):
                c = c0 + slot
                read(c, slot).wait()

                @pl.when(c >= 1)
                def _():
                    write(c - 1, 1 - slot).wait()

                @pl.when(c + 1 < n_chunks)
                def _():
                    read(c + 1, 1 - slot).start()

                write(c, slot).start()

        write(n_chunks - 1, 1).wait()

    return gather_kernel(src, idx)


def _scatter_rows(src, idx, n_out):
    n_src = src.shape[0]
    info = plsc.get_sparse_core_info()
    n_cores = info.num_cores
    n_workers = n_cores * info.num_subcores
    chunk = SCATTER_CHUNK
    per_w = n_src // n_workers
    cpk = per_w // chunk
    n_chunks = 2 * cpk
    assert n_src == per_w * n_workers and per_w == cpk * chunk and idx.shape == (2 * n_src // chunk, chunk)
    mesh = plsc.VectorSubcoreMesh(core_axis_name="core", subcore_axis_name="subcore")

    @pl.kernel(out_type=jax.ShapeDtypeStruct((n_out,) + src.shape[1:], src.dtype), mesh=mesh,
               scratch_types=[pltpu.VMEM((n_chunks, chunk), jnp.int32),
                              pltpu.VMEM((2, chunk) + src.shape[1:], src.dtype),
                              pltpu.SemaphoreType.DMA((2,)),
                              pltpu.SemaphoreType.DMA((2,))],
               name="row_scatter")
    def scatter_kernel(src_hbm, idx_hbm, dst_hbm, idx_v, buf, read_sem, write_sem):
        wid = lax.axis_index("subcore") * n_cores + lax.axis_index("core")
        for k in range(2):
            pltpu.sync_copy(idx_hbm.at[pl.ds(k * (n_src // chunk) + wid * cpk, cpk)],
                            idx_v.at[pl.ds(k * cpk, cpk)])

        def read(q, slot):
            c = lax.rem(q, cpk)
            return pltpu.make_async_copy(src_hbm.at[pl.ds(wid * per_w + c * chunk, chunk)], buf.at[slot],
                                         read_sem.at[slot])

        def write(q, slot):
            return pltpu.make_async_copy(buf.at[slot], dst_hbm.at[idx_v.at[q]], write_sem.at[slot])

        read(0, 0).start()

        @pl.loop(0, n_chunks, step=2)
        def _(q0):
            for slot in range(2):
                q = q0 + slot
                read(q, slot).wait()

                @pl.when(q >= 1)
                def _():
                    write(q - 1, 1 - slot).wait()

                @pl.when(q + 1 < n_chunks)
                def _():
                    read(q + 1, 1 - slot).start()

                write(q, slot).start()

        write(n_chunks - 1, 1).wait()

    return scatter_kernel(src, idx)


def _expert_kernel(be_ref, nused_ref, nvalid_ref, x_ref, wg_ref, wu_ref, wd_ref, y_ref, wg_s, wu_s, wd_s, *, rows):
    i = pl.program_id(0)
    used = i < nused_ref[0]

    @pl.when(used & ((i == 0) | (be_ref[i] != be_ref[jnp.maximum(i - 1, 0)])))
    def _():
        wg_s[...] = wg_ref[0, 0].astype(BF16)
        wu_s[...] = wu_ref[0, 0].astype(BF16)
        wd_s[...] = wd_ref[0, 0].astype(BF16)

    @pl.when(used)
    def _():
        x = _read_token_tiles(x_ref, 0, TILE_ROWS, rows)
        row = lax.broadcasted_iota(jnp.int32, x.shape, 0)
        x = jnp.where(row < nvalid_ref[i], x, 0.0).astype(BF16)
        gate = jnp.dot(x, wg_s[...], preferred_element_type=F32)
        up = jnp.dot(x, wu_s[...], preferred_element_type=F32)
        mid = (gate * jax.nn.sigmoid(gate) * up).astype(BF16)
        _write_token_tiles(y_ref, jnp.dot(mid, wd_s[...], preferred_element_type=F32))

    @pl.when(jnp.logical_not(used))
    def _():
        y_ref[...] = jnp.zeros_like(y_ref)


def _experts(xb, block_e, n_used, n_valid, w_gate, w_up, w_down, layer):
    n_blocks = block_e.shape[0]
    rows = xb.shape[0] // (n_blocks * TILE_ROWS)
    D, F = w_gate.shape[2], w_gate.shape[3]
    tiles = pl.BlockSpec((rows * TILE_ROWS, LANES), lambda i, be, nu, nv: (i, 0))
    return pl.pallas_call(
        functools.partial(_expert_kernel, rows=rows),
        grid_spec=pltpu.PrefetchScalarGridSpec(
            num_scalar_prefetch=3,
            grid=(n_blocks,),
            in_specs=[tiles,
                      pl.BlockSpec((1, 1, D, F), lambda i, be, nu, nv: (layer, be[i], 0, 0)),
                      pl.BlockSpec((1, 1, D, F), lambda i, be, nu, nv: (layer, be[i], 0, 0)),
                      pl.BlockSpec((1, 1, F, D), lambda i, be, nu, nv: (layer, be[i], 0, 0))],
            out_specs=tiles,
            scratch_shapes=[pltpu.VMEM((D, F), BF16), pltpu.VMEM((D, F), BF16), pltpu.VMEM((F, D), BF16)],
        ),
        out_shape=jax.ShapeDtypeStruct(xb.shape, xb.dtype),
        compiler_params=_params("arbitrary"),
        name="moe_experts",
    )(block_e, n_used, n_valid, xb, w_gate, w_up, w_down)


def _dispatch_plan(route, counts, n_tokens):
    A = n_tokens * 2
    counts = counts[0, MOE_GROUPS:MOE_GROUPS + MOE_N_EXPERTS].astype(jnp.int32)
    blocks = (counts + EXPERT_BLOCK - 1) // EXPERT_BLOCK
    block_ends = jnp.cumsum(blocks)
    row_starts = (block_ends - blocks) * EXPERT_BLOCK
    expert = route[:, 0:2].astype(jnp.int32)
    rank = route[:, 4:6].astype(jnp.int32)
    ids = jnp.arange(MOE_N_EXPERTS, dtype=jnp.int32)
    start_of = jnp.sum(jnp.where(expert[:, :, None] == ids, row_starts, 0), axis=-1)
    dest = start_of + rank
    n_blocks = -(-A // EXPERT_BLOCK) + MOE_N_EXPERTS
    block_id = jnp.arange(n_blocks, dtype=jnp.int32)
    block_e = jnp.minimum(jnp.sum(block_ends[None, :] <= block_id[:, None], axis=1),
                          MOE_N_EXPERTS - 1).astype(jnp.int32)
    n_used = block_ends[-1:].astype(jnp.int32)
    mine = block_e[:, None] == ids
    rows_left = jnp.sum(jnp.where(mine, counts + row_starts, 0), axis=1) - block_id * EXPERT_BLOCK
    n_valid = jnp.clip(rows_left, 0, EXPERT_BLOCK).astype(jnp.int32)
    half = 2 * dest.T[:, :, None] + jnp.arange(2, dtype=jnp.int32)
    return half.reshape(-1, SCATTER_CHUNK), dest.reshape(A), block_e, n_used, n_valid


def _moe_dispatch(h, g, w_coarse, b_coarse, w_fine, b_fine, w_gate, w_up, w_down, layer):
    T = h.shape[0]
    xn_tiles, route, counts = _route(h, g, w_coarse, b_coarse, w_fine, b_fine)
    scatter_idx, dest, block_e, n_used, n_valid = _dispatch_plan(route, counts, T)
    n_rows = block_e.shape[0] * EXPERT_BLOCK
    xb = _scatter_rows(xn_tiles.reshape(2 * T, TILE_ROWS // 2, LANES), scatter_idx, 2 * n_rows)
    yb = _experts(xb.reshape(-1, LANES), block_e, n_used, n_valid, w_gate, w_up, w_down, layer)
    yc = _gather_rows(yb.reshape(-1, TILE_ROWS, LANES), dest)
    return yc.reshape(-1, LANES), route


def _ple_kernel(h_ref, yc_ref, route_ref, p_ref, g_ref, wg_ref, wp_ref, gf_ref, o_ref, *, tm, final):
    route = route_ref[...]
    y0 = _read_token_tiles(yc_ref, 0, 2 * TILE_ROWS, tm)
    y1 = _read_token_tiles(yc_ref, TILE_ROWS, 2 * TILE_ROWS, tm)
    h = h_ref[...] + route[:, 2:3] * y0 + route[:, 3:4] * y1
    xn = _rms(h, g_ref[...]).astype(BF16)
    gate = jax.nn.sigmoid(jnp.dot(xn, wg_ref[...], preferred_element_type=F32))
    proj = jnp.dot(p_ref[0].astype(BF16), wp_ref[...], preferred_element_type=F32)
    h = h + gate * proj
    if final:
        h = _rms(h, gf_ref[...])
    o_ref[...] = h


def _combine_ple(h, yc, route, p, layer, g, w_gate, w_proj, g_final, final):
    T, D = h.shape
    tm = min(ROW_TILE, T)
    wg, wp = w_gate.astype(BF16), w_proj.astype(BF16)
    g2, gf = g.reshape(1, D), g_final.reshape(1, D)
    full = lambda a: pl.BlockSpec(a.shape, lambda i: (0,) * a.ndim)
    rows = lambda n: pl.BlockSpec((tm, n), lambda i: (i, 0))
    return pl.pallas_call(
        functools.partial(_ple_kernel, tm=tm, final=final),
        grid=(T // tm,),
        in_specs=[rows(D), pl.BlockSpec((tm * 2 * TILE_ROWS, LANES), lambda i: (i, 0)), rows(LANES),
                  pl.BlockSpec((1, tm, p.shape[2]), lambda i: (layer, i, 0)), full(g2), full(wg), full(wp),
                  full(gf)],
        out_specs=rows(D),
        out_shape=jax.ShapeDtypeStruct((T, D), F32),
        compiler_params=_params("parallel"),
        name="combine_ple",
    )(h, yc, route, p, g2, wg, wp, gf)


def kernel(x, p, norm_mix, norm_ffn, norm_ple, norm_final, gla_w_in, gla_w_gk, gla_b_gk, gla_norm, gla_w_out, pool_w, pool_b, pool_scale, moe_w_coarse, moe_b_coarse, moe_w_fine, moe_b_fine, moe_w_gate, moe_w_up, moe_w_down, ple_w_gate, ple_w_proj):
    batch, seq, d_model = x.shape
    depth = p.shape[0]
    T = batch * seq
    h = x.reshape(T, d_model)
    p3 = p.reshape(depth, T, -1)
    for i in range(depth):
        j = i // 2
        if i % 2 == 0:
            h = _gla_layer(h, norm_mix[i], gla_w_in[j], gla_w_gk[j], gla_b_gk[j], gla_norm[j], gla_w_out[j],
                           batch, seq)
        else:
            h = _pool_layer(h, norm_mix[i], pool_w[j], pool_b[j].reshape(-1), pool_scale[j], batch, seq)
        yc, route = _moe_dispatch(h, norm_ffn[i], moe_w_coarse[i], moe_b_coarse[i], moe_w_fine[i], moe_b_fine[i],
                                  moe_w_gate, moe_w_up, moe_w_down, i)
        h = _combine_ple(h, yc, route, p3, i, norm_ple[i], ple_w_gate[i], ple_w_proj[i],
                         norm_final, final=(i == depth - 1))
    return h.reshape(batch, seq, d_model)
```

```python
import functools

import jax
import jax.numpy as jnp
from jax import lax
from jax.experimental import pallas as pl
from jax.experimental.pallas import tpu as pltpu
from jax.experimental.pallas import tpu_sc as plsc

EPS = 1e-6
GLA_HEADS = 4
GLA_GATE_RANK = 16
GLA_GATE_NORMALIZER = 16.0
GLA_CHUNK = 64
POOL_WINDOWS = (2, 4, 8, 16)
MOE_GROUPS = 8
MOE_EXPERTS_PER_GROUP = 8
MOE_N_EXPERTS = MOE_GROUPS * MOE_EXPERTS_PER_GROUP

LANES = 128
SUBLANES = 8
TILE_ROWS = SUBLANES // 2
VMEM_LIMIT = 56 * 1024 * 1024
ROW_TILE = 512
GLA_BLOCK = 256
EXPERT_BLOCK = 256
GATHER_CHUNK = 64
SCATTER_CHUNK = 128
POOL_HALO = 16

BF16 = jnp.bfloat16
F32 = jnp.float32


def _params(*sem):
    return pltpu.CompilerParams(dimension_semantics=sem, vmem_limit_bytes=VMEM_LIMIT)


def _rms(x, g):
    return x * lax.rsqrt(jnp.mean(x * x, axis=-1, keepdims=True) + EPS) * g


def _read_token_tiles(ref, first, stride, rows):
    chunks = []
    for m in range(TILE_ROWS):
        word = ref[pl.ds(first + m, rows, stride=stride), :]
        chunks.append(pltpu.bitcast(word << 16, F32))
        chunks.append(pltpu.bitcast(word & jnp.uint32(0xFFFF0000), F32))
    return jnp.concatenate(chunks, axis=-1)


def _write_token_tiles(ref, val):
    rows = val.shape[0]
    bits = lambda x: pltpu.bitcast(x.astype(BF16).astype(F32), jnp.uint32)
    for m in range(TILE_ROWS):
        lo = bits(val[:, (2 * m) * LANES:(2 * m + 1) * LANES])
        hi = bits(val[:, (2 * m + 1) * LANES:(2 * m + 2) * LANES])
        ref[pl.ds(m, rows, stride=TILE_ROWS), :] = (lo >> 16) | hi


def _gla_proj_kernel(h_ref, g_ref, wq_ref, wk_ref, wv_ref, wr_ref, wlr_ref, wgk_ref, bgk_ref,
                     q_ref, k_ref, v_ref, r_ref, gk_ref, *, q_scale):
    xn = _rms(h_ref[...], g_ref[...]).astype(BF16)
    q_ref[...] = (jnp.dot(xn, wq_ref[...], preferred_element_type=F32) * q_scale).astype(BF16)
    k_ref[...] = jnp.dot(xn, wk_ref[...], preferred_element_type=F32).astype(BF16)
    v_ref[...] = jnp.dot(xn, wv_ref[...], preferred_element_type=F32).astype(BF16)
    r_ref[...] = jnp.dot(xn, wr_ref[...], preferred_element_type=F32).astype(BF16)
    lr = jnp.dot(xn, wlr_ref[...], preferred_element_type=F32).astype(BF16)
    z = jnp.dot(lr, wgk_ref[...], preferred_element_type=F32) + bgk_ref[...]
    gk_ref[...] = (jnp.minimum(z, 0.0) - jnp.log(1.0 + jnp.exp(-jnp.abs(z)))) * (1.0 / GLA_GATE_NORMALIZER)


def _gla_proj(h, g, w_in, w_gk, b_gk):
    T, D = h.shape
    K = w_gk.shape[1]
    V = (w_in.shape[1] - GLA_GATE_RANK - 2 * K) // 2
    wq = w_in[:, :K].astype(BF16)
    wk = w_in[:, K:2 * K].astype(BF16)
    wv = w_in[:, 2 * K:2 * K + V].astype(BF16)
    wr = w_in[:, 2 * K + V:2 * K + 2 * V].astype(BF16)
    wlr = jnp.pad(w_in[:, 2 * K + 2 * V:], ((0, 0), (0, LANES - GLA_GATE_RANK))).astype(BF16)
    wgk = jnp.pad(w_gk, ((0, LANES - GLA_GATE_RANK), (0, 0))).astype(BF16)
    tm = min(ROW_TILE, T)
    full = lambda a: pl.BlockSpec(a.shape, lambda i: (0,) * a.ndim)
    rows = lambda n: pl.BlockSpec((tm, n), lambda i: (i, 0))
    g2 = g.reshape(1, D)
    b2 = b_gk.reshape(1, K)
    return pl.pallas_call(
        functools.partial(_gla_proj_kernel, q_scale=float((K // GLA_HEADS) ** -0.5)),
        grid=(T // tm,),
        in_specs=[rows(D), full(g2), full(wq), full(wk), full(wv), full(wr), full(wlr), full(wgk), full(b2)],
        out_specs=[rows(K), rows(K), rows(V), rows(V), rows(K)],
        out_shape=[jax.ShapeDtypeStruct((T, K), BF16), jax.ShapeDtypeStruct((T, K), BF16),
                   jax.ShapeDtypeStruct((T, V), BF16), jax.ShapeDtypeStruct((T, V), BF16),
                   jax.ShapeDtypeStruct((T, K), F32)],
        compiler_params=_params("parallel"),
        name="gla_proj",
    )(h, g2, wq, wk, wv, wr, wlr, wgk, b2)


def _gla_core_kernel(tri_ref, q_ref, k_ref, v_ref, r_ref, gk_ref, ng_ref, o_ref, state_ref, *, dk, dv):
    @pl.when(pl.program_id(1) == 0)
    def _():
        state_ref[...] = jnp.zeros_like(state_ref)

    C = GLA_CHUNK
    n_chunks = q_ref.shape[0] // C
    gk = gk_ref[...]
    gk_hi = gk.astype(BF16)
    gk_lo = (gk - gk_hi.astype(F32)).astype(BF16)
    tri = tri_ref[...]
    b_all = (jnp.dot(tri, gk_hi, preferred_element_type=F32)
             + jnp.dot(tri, gk_lo, preferred_element_type=F32))
    row = lax.broadcasted_iota(jnp.int32, (C, C), 0)
    col = lax.broadcasted_iota(jnp.int32, (C, C), 1)
    causal = col <= row
    ng = ng_ref[...]
    for hd in range(GLA_HEADS):
        ks = slice(hd * dk, (hd + 1) * dk)
        vs = slice(hd * dv, (hd + 1) * dv)
        for c in range(n_chunks):
            rs = slice(c * C, (c + 1) * C)
            b = b_all[rs, ks]
            b_last = b[C - 1:C, :]
            q = q_ref[rs, ks].astype(F32)
            k = k_ref[rs, ks].astype(F32)
            v = v_ref[rs, vs]
            q_dec = (q * jnp.exp(b)).astype(BF16)
            k_inv = (k * jnp.exp(-b)).astype(BF16)
            k_end = (k * jnp.exp(b_last - b)).astype(BF16)
            attn = lax.dot_general(q_dec, k_inv, (((1,), (1,)), ((), ())), preferred_element_type=F32)
            attn = jnp.where(causal, attn, 0.0).astype(BF16)
            st = state_ref[hd]
            o = jnp.dot(attn, v, preferred_element_type=F32)
            o = o + lax.dot_general(q_dec, st.astype(BF16), (((1,), (1,)), ((), ())),
                                    preferred_element_type=F32)
            upd = lax.dot_general(v, k_end, (((0,), (0,)), ((), ())), preferred_element_type=F32)
            state_ref[hd] = st * jnp.exp(b_last) + upd
            rr = r_ref[rs, vs].astype(F32)
            o_ref[rs, vs] = (_rms(o, ng) * (rr * jax.nn.sigmoid(rr))).astype(BF16)


def _gla_core(q, k, v, r, gk, norm_g, batch, seq):
    T, K = q.shape
    V = v.shape[1]
    dk, dv = K // GLA_HEADS, V // GLA_HEADS
    L = min(GLA_BLOCK, seq)
    nblk = seq // L
    idx = jnp.arange(L)
    tri = ((idx[None, :] <= idx[:, None]) & (idx[None, :] // GLA_CHUNK == idx[:, None] // GLA_CHUNK)).astype(BF16)
    ng = norm_g.reshape(1, dv)
    rows = lambda n: pl.BlockSpec((L, n), lambda b, s: (b * nblk + s, 0))
    full = lambda a: pl.BlockSpec(a.shape, lambda b, s: (0,) * a.ndim)
    return pl.pallas_call(
        functools.partial(_gla_core_kernel, dk=dk, dv=dv),
        grid=(batch, nblk),
        in_specs=[full(tri), rows(K), rows(K), rows(V), rows(V), rows(K), full(ng)],
        out_specs=rows(V),
        out_shape=jax.ShapeDtypeStruct((T, V), BF16),
        scratch_shapes=[pltpu.VMEM((GLA_HEADS, dv, dk), F32)],
        compiler_params=_params("parallel", "arbitrary"),
        name="gla_core",
    )(tri, q, k, v, r, gk, ng)


def _proj_residual_kernel(h_ref, x_ref, w_ref, o_ref):
    o_ref[...] = h_ref[...] + jnp.dot(x_ref[...], w_ref[...], preferred_element_type=F32)


def _proj_residual(h, x, w):
    T, D = h.shape
    tm = min(ROW_TILE, T)
    wb = w.astype(BF16)
    return pl.pallas_call(
        _proj_residual_kernel,
        grid=(T // tm,),
        in_specs=[pl.BlockSpec((tm, D), lambda i: (i, 0)), pl.BlockSpec((tm, x.shape[1]), lambda i: (i, 0)),
                  pl.BlockSpec(wb.shape, lambda i: (0, 0))],
        out_specs=pl.BlockSpec((tm, D), lambda i: (i, 0)),
        out_shape=jax.ShapeDtypeStruct((T, D), F32),
        compiler_params=_params("parallel"),
        name="gla_out_proj",
    )(h, x, wb)


def _gla_layer(h, g, w_in, w_gk, b_gk, norm_g, w_out, batch, seq):
    q, k, v, r, gk = _gla_proj(h, g, w_in, w_gk, b_gk)
    og = _gla_core(q, k, v, r, gk, norm_g, batch, seq)
    return _proj_residual(h, og, w_out)


def _pool_kernel(h_ref, g_ref, w_ref, b_ref, sc_ref, o_ref, carry_ref, *, ts):
    s = pl.program_id(1)

    @pl.when(s == 0)
    def _():
        carry_ref[...] = jnp.zeros_like(carry_ref)

    h = h_ref[...]
    xn = _rms(h, g_ref[...])
    ext = jnp.concatenate([carry_ref[...], xn], axis=0)
    carry_ref[...] = xn[ts - POOL_HALO:, :]
    gd = w_ref.shape[1]
    t = s * ts + lax.broadcasted_iota(jnp.int32, (ts, gd), 0)
    for gi, win in enumerate(POOL_WINDOWS):
        cols = slice(gi * gd, (gi + 1) * gd)
        acc = ext[:, cols]
        d = 1
        while d < win:
            acc = acc[d:, :] + acc[:-d, :]
            d *= 2
        start = POOL_HALO - (win - 1)
        wsum = acc[start:start + ts, :]
        cnt = jnp.minimum(t + 1, win).astype(F32)
        pooled = (wsum / cnt - xn[:, cols]).astype(BF16)
        y = jnp.dot(pooled, w_ref[gi], preferred_element_type=F32) + b_ref[:, cols]
        o_ref[:, cols] = h[:, cols] + y * sc_ref[:, cols]


def _pool_layer(h, g, w, b, scale, batch, seq):
    T, D = h.shape
    ts = min(ROW_TILE, seq)
    nblk = seq // ts
    wb = w.astype(BF16)
    g2, b2, s2 = g.reshape(1, D), b.reshape(1, D), scale.reshape(1, D)
    rows = pl.BlockSpec((ts, D), lambda bb, s: (bb * nblk + s, 0))
    full = lambda a: pl.BlockSpec(a.shape, lambda bb, s: (0,) * a.ndim)
    return pl.pallas_call(
        functools.partial(_pool_kernel, ts=ts),
        grid=(batch, nblk),
        in_specs=[rows, full(g2), full(wb), full(b2), full(s2)],
        out_specs=rows,
        out_shape=jax.ShapeDtypeStruct((T, D), F32),
        scratch_shapes=[pltpu.VMEM((POOL_HALO, D), F32)],
        compiler_params=_params("parallel", "arbitrary"),
        name="pool_mixer",
    )(h, g2, wb, b2, s2)


def _route_kernel(h_ref, g_ref, whi_ref, wlo_ref, b_ref, ltri_ref, xn_ref, route_ref, count_ref):
    @pl.when(pl.program_id(0) == 0)
    def _():
        count_ref[...] = jnp.zeros_like(count_ref)

    xn = _rms(h_ref[...], g_ref[...])
    _write_token_tiles(xn_ref, xn)
    x_hi = xn.astype(BF16)
    x_lo = (xn - x_hi.astype(F32)).astype(BF16)
    logits = (jnp.dot(x_hi, whi_ref[...], preferred_element_type=F32)
              + jnp.dot(x_hi, wlo_ref[...], preferred_element_type=F32)
              + jnp.dot(x_lo, whi_ref[...], preferred_element_type=F32)) + b_ref[...]
    lane = lax.broadcasted_iota(jnp.int32, logits.shape, 1)
    neg = jnp.float32(-jnp.inf)
    big = jnp.int32(LANES)
    is_c = lane < MOE_GROUPS
    lc = jnp.where(is_c, logits, neg)
    cmax = jnp.max(lc, axis=-1, keepdims=True)
    g_top = jnp.min(jnp.where(lc == cmax, lane, big), axis=-1, keepdims=True)
    p_g = 1.0 / jnp.sum(jnp.where(is_c, jnp.exp(logits - cmax), 0.0), axis=-1, keepdims=True)
    fine_lane = lane - MOE_GROUPS
    in_grp = (fine_lane >= g_top * MOE_EXPERTS_PER_GROUP) & (fine_lane < (g_top + 1) * MOE_EXPERTS_PER_GROUP)
    lf = jnp.where(in_grp, logits, neg)
    m1 = jnp.max(lf, axis=-1, keepdims=True)
    i1 = jnp.min(jnp.where(lf == m1, lane, big), axis=-1, keepdims=True)
    lf2 = jnp.where(lane == i1, neg, lf)
    m2 = jnp.max(lf2, axis=-1, keepdims=True)
    i2 = jnp.min(jnp.where(lf2 == m2, lane, big), axis=-1, keepdims=True)
    e2 = jnp.exp(m2 - m1)
    w1 = p_g / (1.0 + e2)
    w2 = p_g * e2 / (1.0 + e2)
    hot1 = lane == i1
    hot2 = lane == i2
    ltri = ltri_ref[...]
    before1 = jnp.dot(ltri, hot1.astype(BF16), preferred_element_type=F32)
    before2 = jnp.dot(ltri, hot2.astype(BF16), preferred_element_type=F32)
    base = count_ref[...]
    total1 = jnp.sum(hot1.astype(F32), axis=0, keepdims=True)
    total2 = jnp.sum(hot2.astype(F32), axis=0, keepdims=True)
    r1 = jnp.sum(jnp.where(hot1, before1 + base, 0.0), axis=-1, keepdims=True)
    r2 = jnp.sum(jnp.where(hot2, before2 + (base + total1), 0.0), axis=-1, keepdims=True)
    count_ref[...] = base + total1 + total2
    out = jnp.where(lane == 0, (i1 - MOE_GROUPS).astype(F32),
                    jnp.where(lane == 1, (i2 - MOE_GROUPS).astype(F32),
                              jnp.where(lane == 2, w1,
                                        jnp.where(lane == 3, w2,
                                                  jnp.where(lane == 4, r1, jnp.where(lane == 5, r2, 0.0))))))
    route_ref[...] = out


def _route(h, g, w_coarse, b_coarse, w_fine, b_fine):
    T, D = h.shape
    n_log = MOE_GROUPS + MOE_N_EXPERTS
    wr = jnp.pad(jnp.concatenate([w_coarse, w_fine], axis=1), ((0, 0), (0, LANES - n_log)))
    wr_hi = wr.astype(BF16)
    wr_lo = (wr - wr_hi.astype(F32)).astype(BF16)
    br = jnp.pad(jnp.concatenate([b_coarse, b_fine]), (0, LANES - n_log)).reshape(1, LANES)
    g2 = g.reshape(1, D)
    tm = min(ROW_TILE, T)
    idx = jnp.arange(tm)
    ltri = (idx[None, :] < idx[:, None]).astype(BF16)
    full = lambda a: pl.BlockSpec(a.shape, lambda i: (0,) * a.ndim)
    return pl.pallas_call(
        _route_kernel,
        grid=(T // tm,),
        in_specs=[pl.BlockSpec((tm, D), lambda i: (i, 0)), full(g2), full(wr_hi), full(wr_lo), full(br),
                  full(ltri)],
        out_specs=[pl.BlockSpec((tm * TILE_ROWS, LANES), lambda i: (i, 0)),
                   pl.BlockSpec((tm, LANES), lambda i: (i, 0)),
                   pl.BlockSpec((1, LANES), lambda i: (0, 0))],
        out_shape=[jax.ShapeDtypeStruct((T * TILE_ROWS, LANES), jnp.uint32), jax.ShapeDtypeStruct((T, LANES), F32),
                   jax.ShapeDtypeStruct((1, LANES), F32)],
        compiler_params=_params("arbitrary"),
        name="moe_route",
    )(h, g2, wr_hi, wr_lo, br, ltri)


def _gather_rows(src, idx):
    n = idx.shape[0]
    info = plsc.get_sparse_core_info()
    n_cores = info.num_cores
    n_workers = n_cores * info.num_subcores
    per_w = n // n_workers
    chunk = GATHER_CHUNK
    n_chunks = per_w // chunk
    assert n == per_w * n_workers and per_w == n_chunks * chunk and n_chunks % 2 == 0
    mesh = plsc.VectorSubcoreMesh(core_axis_name="core", subcore_axis_name="subcore")

    @pl.kernel(out_type=jax.ShapeDtypeStruct((n,) + src.shape[1:], src.dtype), mesh=mesh,
               scratch_types=[pltpu.VMEM((per_w,), jnp.int32),
                              pltpu.VMEM((2, chunk) + src.shape[1:], src.dtype),
                              pltpu.SemaphoreType.DMA((2,)),
                              pltpu.SemaphoreType.DMA((2,))],
               name="row_gather")
    def gather_kernel(src_hbm, idx_hbm, dst_hbm, idx_v, buf, read_sem, write_sem):
        base = (lax.axis_index("subcore") * n_cores + lax.axis_index("core")) * per_w
        pltpu.sync_copy(idx_hbm.at[pl.ds(base, per_w)], idx_v)

        def read(c, slot):
            return pltpu.make_async_copy(src_hbm.at[idx_v.at[pl.ds(c * chunk, chunk)]], buf.at[slot],
                                         read_sem.at[slot])

        def write(c, slot):
            return pltpu.make_async_copy(buf.at[slot], dst_hbm.at[pl.ds(base + c * chunk, chunk)],
                                         write_sem.at[slot])

        read(0, 0).start()

        @pl.loop(0, n_chunks, step=2)
        def _(c0):
            for slot in range(2):
                c = c0 + slot
                read(c, slot).wait()

                @pl.when(c >= 1)
                def _():
                    write(c - 1, 1 - slot).wait()

                @pl.when(c + 1 < n_chunks)
                def _():
                    read(c + 1, 1 - slot).start()

                write(c, slot).start()

        write(n_chunks - 1, 1).wait()

    return gather_kernel(src, idx)


def _scatter_rows(src, idx, n_out):
    n_src = src.shape[0]
    info = plsc.get_sparse_core_info()
    n_cores = info.num_cores
    n_workers = n_cores * info.num_subcores
    chunk = SCATTER_CHUNK
    per_w = n_src // n_workers
    cpk = per_w // chunk
    n_chunks = 2 * cpk
    assert n_src == per_w * n_workers and per_w == cpk * chunk and idx.shape == (2 * n_src // chunk, chunk)
    mesh = plsc.VectorSubcoreMesh(core_axis_name="core", subcore_axis_name="subcore")

    @pl.kernel(out_type=jax.ShapeDtypeStruct((n_out,) + src.shape[1:], src.dtype), mesh=mesh,
               scratch_types=[pltpu.VMEM((n_chunks, chunk), jnp.int32),
                              pltpu.VMEM((2, chunk) + src.shape[1:], src.dtype),
                              pltpu.SemaphoreType.DMA((2,)),
                              pltpu.SemaphoreType.DMA((2,))],
               name="row_scatter")
    def scatter_kernel(src_hbm, idx_hbm, dst_hbm, idx_v, buf, read_sem, write_sem):
        wid = lax.axis_index("subcore") * n_cores + lax.axis_index("core")
        for k in range(2):
            pltpu.sync_copy(idx_hbm.at[pl.ds(k * (n_src // chunk) + wid * cpk, cpk)],
                            idx_v.at[pl.ds(k * cpk, cpk)])

        def read(q, slot):
            c = lax.rem(q, cpk)
            return pltpu.make_async_copy(src_hbm.at[pl.ds(wid * per_w + c * chunk, chunk)], buf.at[slot],
                                         read_sem.at[slot])

        def write(q, slot):
            return pltpu.make_async_copy(buf.at[slot], dst_hbm.at[idx_v.at[q]], write_sem.at[slot])

        read(0, 0).start()

        @pl.loop(0, n_chunks, step=2)
        def _(q0):
            for slot in range(2):
                q = q0 + slot
                read(q, slot).wait()

                @pl.when(q >= 1)
                def _():
                    write(q - 1, 1 - slot).wait()

                @pl.when(q + 1 < n_chunks)
                def _():
                    read(q + 1, 1 - slot).start()

                write(q, slot).start()

        write(n_chunks - 1, 1).wait()

    return scatter_kernel(src, idx)


def _expert_kernel(be_ref, nused_ref, nvalid_ref, x_ref, wg_ref, wu_ref, wd_ref, y_ref, wg_s, wu_s, wd_s, *, rows):
    i = pl.program_id(0)
    used = i < nused_ref[0]

    @pl.when(used & ((i == 0) | (be_ref[i] != be_ref[jnp.maximum(i - 1, 0)])))
    def _():
        wg_s[...] = wg_ref[0, 0].astype(BF16)
        wu_s[...] = wu_ref[0, 0].astype(BF16)
        wd_s[...] = wd_ref[0, 0].astype(BF16)

    @pl.when(used)
    def _():
        x = _read_token_tiles(x_ref, 0, TILE_ROWS, rows)
        row = lax.broadcasted_iota(jnp.int32, x.shape, 0)
        x = jnp.where(row < nvalid_ref[i], x, 0.0).astype(BF16)
        gate = jnp.dot(x, wg_s[...], preferred_element_type=F32)
        up = jnp.dot(x, wu_s[...], preferred_element_type=F32)
        mid = (gate * jax.nn.sigmoid(gate) * up).astype(BF16)
        _write_token_tiles(y_ref, jnp.dot(mid, wd_s[...], preferred_element_type=F32))

    @pl.when(jnp.logical_not(used))
    def _():
        y_ref[...] = jnp.zeros_like(y_ref)


def _experts(xb, block_e, n_used, n_valid, w_gate, w_up, w_down, layer):
    n_blocks = block_e.shape[0]
    rows = xb.shape[0] // (n_blocks * TILE_ROWS)
    D, F = w_gate.shape[2], w_gate.shape[3]
    tiles = pl.BlockSpec((rows * TILE_ROWS, LANES), lambda i, be, nu, nv: (i, 0))
    return pl.pallas_call(
        functools.partial(_expert_kernel, rows=rows),
        grid_spec=pltpu.PrefetchScalarGridSpec(
            num_scalar_prefetch=3,
            grid=(n_blocks,),
            in_specs=[tiles,
                      pl.BlockSpec((1, 1, D, F), lambda i, be, nu, nv: (layer, be[i], 0, 0)),
                      pl.BlockSpec((1, 1, D, F), lambda i, be, nu, nv: (layer, be[i], 0, 0)),
                      pl.BlockSpec((1, 1, F, D), lambda i, be, nu, nv: (layer, be[i], 0, 0))],
            out_specs=tiles,
            scratch_shapes=[pltpu.VMEM((D, F), BF16), pltpu.VMEM((D, F), BF16), pltpu.VMEM((F, D), BF16)],
        ),
        out_shape=jax.ShapeDtypeStruct(xb.shape, xb.dtype),
        compiler_params=_params("arbitrary"),
        name="moe_experts",
    )(block_e, n_used, n_valid, xb, w_gate, w_up, w_down)


def _dispatch_plan(route, counts, n_tokens):
    A = n_tokens * 2
    counts = counts[0, MOE_GROUPS:MOE_GROUPS + MOE_N_EXPERTS].astype(jnp.int32)
    blocks = (counts + EXPERT_BLOCK - 1) // EXPERT_BLOCK
    block_ends = jnp.cumsum(blocks)
    row_starts = (block_ends - blocks) * EXPERT_BLOCK
    expert = route[:, 0:2].astype(jnp.int32)
    rank = route[:, 4:6].astype(jnp.int32)
    ids = jnp.arange(MOE_N_EXPERTS, dtype=jnp.int32)
    start_of = jnp.sum(jnp.where(expert[:, :, None] == ids, row_starts, 0), axis=-1)
    dest = start_of + rank
    n_blocks = -(-A // EXPERT_BLOCK) + MOE_N_EXPERTS
    block_id = jnp.arange(n_blocks, dtype=jnp.int32)
    block_e = jnp.minimum(jnp.sum(block_ends[None, :] <= block_id[:, None], axis=1),
                          MOE_N_EXPERTS - 1).astype(jnp.int32)
    n_used = block_ends[-1:].astype(jnp.int32)
    mine = block_e[:, None] == ids
    rows_left = jnp.sum(jnp.where(mine, counts + row_starts, 0), axis=1) - block_id * EXPERT_BLOCK
    n_valid = jnp.clip(rows_left, 0, EXPERT_BLOCK).astype(jnp.int32)
    half = 2 * dest.T[:, :, None] + jnp.arange(2, dtype=jnp.int32)
    return half.reshape(-1, SCATTER_CHUNK), dest.reshape(A), block_e, n_used, n_valid


def _moe_dispatch(h, g, w_coarse, b_coarse, w_fine, b_fine, w_gate, w_up, w_down, layer):
    T = h.shape[0]
    xn_tiles, route, counts = _route(h, g, w_coarse, b_coarse, w_fine, b_fine)
    scatter_idx, dest, block_e, n_used, n_valid = _dispatch_plan(route, counts, T)
    n_rows = block_e.shape[0] * EXPERT_BLOCK
    xb = _scatter_rows(xn_tiles.reshape(2 * T, TILE_ROWS // 2, LANES), scatter_idx, 2 * n_rows)
    yb = _experts(xb.reshape(-1, LANES), block_e, n_used, n_valid, w_gate, w_up, w_down, layer)
    yc = _gather_rows(yb.reshape(-1, TILE_ROWS, LANES), dest)
    return yc.reshape(-1, LANES), route


def _ple_kernel(h_ref, yc_ref, route_ref, p_ref, g_ref, wg_ref, wp_ref, gf_ref, o_ref, *, tm, final):
    route = route_ref[...]
    y0 = _read_token_tiles(yc_ref, 0, 2 * TILE_ROWS, tm)
    y1 = _read_token_tiles(yc_ref, TILE_ROWS, 2 * TILE_ROWS, tm)
    h = h_ref[...] + route[:, 2:3] * y0 + route[:, 3:4] * y1
    xn = _rms(h, g_ref[...]).astype(BF16)
    gate = jax.nn.sigmoid(jnp.dot(xn, wg_ref[...], preferred_element_type=F32))
    proj = jnp.dot(p_ref[0].astype(BF16), wp_ref[...], preferred_element_type=F32)
    h = h + gate * proj
    if final:
        h = _rms(h, gf_ref[...])
    o_ref[...] = h


def _combine_ple(h, yc, route, p, layer, g, w_gate, w_proj, g_final, final):
    T, D = h.shape
    tm = min(ROW_TILE, T)
    wg, wp = w_gate.astype(BF16), w_proj.astype(BF16)
    g2, gf = g.reshape(1, D), g_final.reshape(1, D)
    full = lambda a: pl.BlockSpec(a.shape, lambda i: (0,) * a.ndim)
    rows = lambda n: pl.BlockSpec((tm, n), lambda i: (i, 0))
    return pl.pallas_call(
        functools.partial(_ple_kernel, tm=tm, final=final),
        grid=(T // tm,),
        in_specs=[rows(D), pl.BlockSpec((tm * 2 * TILE_ROWS, LANES), lambda i: (i, 0)), rows(LANES),
                  pl.BlockSpec((1, tm, p.shape[2]), lambda i: (layer, i, 0)), full(g2), full(wg), full(wp),
                  full(gf)],
        out_specs=rows(D),
        out_shape=jax.ShapeDtypeStruct((T, D), F32),
        compiler_params=_params("parallel"),
        name="combine_ple",
    )(h, yc, route, p, g2, wg, wp, gf)


def kernel(x, p, norm_mix, norm_ffn, norm_ple, norm_final, gla_w_in, gla_w_gk, gla_b_gk, gla_norm, gla_w_out, pool_w, pool_b, pool_scale, moe_w_coarse, moe_b_coarse, moe_w_fine, moe_b_fine, moe_w_gate, moe_w_up, moe_w_down, ple_w_gate, ple_w_proj):
    batch, seq, d_model = x.shape
    depth = p.shape[0]
    T = batch * seq
    h = x.reshape(T, d_model)
    p3 = p.reshape(depth, T, -1)
    for i in range(depth):
        j = i // 2
        if i % 2 == 0:
            h = _gla_layer(h, norm_mix[i], gla_w_in[j], gla_w_gk[j], gla_b_gk[j], gla_norm[j], gla_w_out[j],
                           batch, seq)
        else:
            h = _pool_layer(h, norm_mix[i], pool_w[j], pool_b[j].reshape(-1), pool_scale[j], batch, seq)
        yc, route = _moe_dispatch(h, norm_ffn[i], moe_w_coarse[i], moe_b_coarse[i], moe_w_fine[i], moe_b_fine[i],
                                  moe_w_gate, moe_w_up, moe_w_down, i)
        h = _combine_ple(h, yc, route, p3, i, norm_ple[i], ple_w_gate[i], ple_w_proj[i],
                         norm_final, final=(i == depth - 1))
    return h.reshape(batch, seq, d_model)
```
